```python
import math
import jax, jax.numpy as jnp
from jax import lax
import numpy as np

D_MODEL = 1024
BATCH = 32
SEQ = 256
DEPTH = 2
DEC_BATCH = 4
DEC_SEQ = 1024
PAST_LEN = 256

GRID_W = 64
CHUNK = 128
MLSTM_HEADS = 4
MLSTM_HD = D_MODEL // (2 * MLSTM_HEADS)
MLSTM_W = MLSTM_HEADS * MLSTM_HD
N_GATE_COLS = 4 * MLSTM_HEADS
CONV_W = D_MODEL // 2
CONV_K = 31
CONV_PAD = CONV_K // 2
RET_HEADS = 4
RET_HD = D_MODEL // (2 * RET_HEADS)
RET_W = RET_HEADS * RET_HD
N_BRANCH = 3
FFN_HIDDEN = -(-8 * D_MODEL // (3 * 256)) * 256
ROPE_BASE = 10000.0
LN_EPS = 1e-5
DEEPNORM_ALPHA = (2.0 * DEPTH) ** 0.25
DEEPNORM_BETA = (8.0 * DEPTH) ** -0.25
IN_SIZES = (MLSTM_W, MLSTM_W, MLSTM_W, MLSTM_W, N_GATE_COLS, CONV_W, CONV_W,
            RET_W, RET_W, RET_W, RET_W, N_BRANCH * D_MODEL)
IN_WIDTH = sum(IN_SIZES)

kernel_name = 'hybrid_mlstm_conformer_retention_diffusion_step'


def layer_norm(x, w, b):
    xf = x.astype(jnp.float32)
    mu = jnp.mean(xf, -1, keepdims=True)
    var = jnp.mean(jnp.square(xf - mu), -1, keepdims=True)
    return ((xf - mu) * lax.rsqrt(var + LN_EPS) * w.astype(jnp.float32) + b.astype(jnp.float32)).astype(x.dtype)


def head_norm(y, w):
    mu = jnp.mean(y, -1, keepdims=True)
    var = jnp.mean(jnp.square(y - mu), -1, keepdims=True)
    yn = (y - mu) * lax.rsqrt(var + LN_EPS)
    return yn.reshape(y.shape[0], y.shape[1], -1) * w.astype(jnp.float32)


def _flip(a):
    return jnp.flip(a, axis=1)


def _to_chunks(a):
    b, t = a.shape[0], a.shape[1]
    return jnp.moveaxis(a.reshape((b, t // CHUNK, CHUNK) + a.shape[2:]), 1, 0)


def _from_chunks(a):
    a = jnp.moveaxis(a, 0, 1)
    return a.reshape((a.shape[0], a.shape[1] * a.shape[2]) + a.shape[3:])


def rope_2d(x):
    t = x.shape[1]
    n_rows = t // GRID_W
    rows, cols = jnp.meshgrid(jnp.arange(n_rows, dtype=jnp.float32),
                              jnp.arange(GRID_W, dtype=jnp.float32), indexing='ij')
    n_pairs = x.shape[-1] // 4
    freqs = ROPE_BASE ** (-jnp.arange(n_pairs, dtype=jnp.float32) / n_pairs)
    ang = jnp.concatenate([rows.reshape(-1)[:, None] * freqs, cols.reshape(-1)[:, None] * freqs], -1)
    cos = jnp.cos(ang)[None, :, None, :]
    sin = jnp.sin(ang)[None, :, None, :]
    x1, x2 = x[..., 0::2], x[..., 1::2]
    return jnp.stack([x1 * cos - x2 * sin, x1 * sin + x2 * cos], -1).reshape(x.shape)


def mlstm_chunked(q, k, v, ig, lf, state):
    causal = jnp.tril(jnp.ones((CHUNK, CHUNK), dtype=bool))

    def step(carry, xs):
        c0, n0, m0 = carry
        qc, kc, vc, igc, lfc = xs
        igc = jnp.transpose(igc, (0, 2, 1))
        b = jnp.cumsum(jnp.transpose(lfc, (0, 2, 1)), axis=-1)
        log_d = jnp.where(causal, b[..., :, None] - b[..., None, :] + igc[..., None, :], -jnp.inf)
        inter = b + m0[..., None]
        m = jnp.maximum(inter, jnp.max(log_d, -1))
        dmat = jnp.exp(log_d - m[..., None])
        s = jnp.einsum('bihd,bjhd->bhij', qc, kc) * dmat
        w_inter = jnp.exp(inter - m)
        num = jnp.einsum('bhij,bjhe->bhie', s, vc) + jnp.einsum('bihd,bhde->bhie', qc, c0) * w_inter[..., None]
        den = jnp.sum(s, -1) + jnp.einsum('bihd,bhd->bhi', qc, n0) * w_inter
        h = num / jnp.maximum(jnp.abs(den), jnp.exp(-m))[..., None]
        b_last = b[..., -1]
        log_w = b_last[..., None] - b + igc
        m_new = jnp.maximum(b_last + m0, jnp.max(log_w, -1))
        w = jnp.exp(log_w - m_new[..., None])
        carry_scale = jnp.exp(b_last + m0 - m_new)
        c_new = c0 * carry_scale[..., None, None] + jnp.einsum('bhj,bjhd,bjhe->bhde', w, kc, vc)
        n_new = n0 * carry_scale[..., None] + jnp.einsum('bhj,bjhd->bhd', w, kc)
        return (c_new, n_new, m_new), jnp.transpose(h, (0, 2, 1, 3))

    final, hs = lax.scan(step, state, (_to_chunks(q), _to_chunks(k), _to_chunks(v), _to_chunks(ig), _to_chunks(lf)))
    return _from_chunks(hs), final


def retention_chunked(q, k, v, log_gamma, s0):
    i = jnp.arange(CHUNK, dtype=jnp.float32)
    diff = i[:, None] - i[None, :]
    decay = jnp.where(diff >= 0, jnp.exp(log_gamma[:, None, None] * jnp.maximum(diff, 0.0)), 0.0)
    xi = jnp.exp(log_gamma[None, :] * (i[:, None] + 1.0))
    zeta = jnp.exp(log_gamma[None, :] * (CHUNK - 1.0 - i[:, None]))
    chunk_decay = jnp.exp(log_gamma * CHUNK)

    def step(s, xs):
        qc, kc, vc = xs
        scores = jnp.einsum('bihd,bjhd->bhij', qc, kc) * decay
        y = jnp.einsum('bhij,bjhe->bihe', scores, vc) + jnp.einsum('bihd,bhde->bihe', qc, s) * xi[None, :, :, None]
        s = s * chunk_decay[None, :, None, None] + jnp.einsum('bjhd,bjhe->bhde', kc * zeta[None, :, :, None], vc)
        return s, y

    s_final, ys = lax.scan(step, s0, (_to_chunks(q), _to_chunks(k), _to_chunks(v)))
    return _from_chunks(ys), s_final


def mixing_sublayer(h, p, init, latent):
    f32 = jnp.float32
    bsz, t, _ = h.shape
    offsets = np.cumsum(IN_SIZES)[:-1].tolist()
    (mq, mk, mv, mo, mg, ca, cg, rq, rk, rv, rg, gm) = jnp.split(h @ p['in_w'], offsets, axis=-1)

    def heads(a, n):
        return a.reshape(bsz, t, n, -1).astype(f32)

    c0, n0, m0, s0 = (s.astype(f32) for s in init)

    q = heads(mq, MLSTM_HEADS)
    k = heads(mk, MLSTM_HEADS) * MLSTM_HD ** -0.5
    v = heads(mv, MLSTM_HEADS)
    gates = mg.reshape(bsz, t, 4, MLSTM_HEADS).astype(f32) + p['mlstm_gate_b'].astype(f32)
    ig_f, lf_f = gates[:, :, 0], jax.nn.log_sigmoid(gates[:, :, 1])
    ig_b, lf_b = gates[:, :, 2], jax.nn.log_sigmoid(gates[:, :, 3])
    h_f, (cf, nf, mf) = mlstm_chunked(q, k, v, ig_f, lf_f, (c0[:, 0], n0[:, 0], m0[:, 0]))
    h_b, (cb, nb, mb) = mlstm_chunked(_flip(q), _flip(k), _flip(v), _flip(ig_b), _flip(lf_b),
                                      (c0[:, 1], n0[:, 1], m0[:, 1]))
    h_a = head_norm(h_f + _flip(h_b), p['mlstm_norm_w']).astype(h.dtype) * jax.nn.sigmoid(mo)
    y_a = h_a @ p['mlstm_out_w']

    u = ca * jax.nn.sigmoid(cg)
    u = lax.conv_general_dilated(u, p['conv_w'][:, None, :], window_strides=(1,), padding=[(CONV_PAD, CONV_PAD)],
                                 dimension_numbers=('NWC', 'WIO', 'NWC'), feature_group_count=CONV_W) + p['conv_b']
    u = jax.nn.silu(layer_norm(u, p['conv_ln_w'], p['conv_ln_b']))
    y_b = u @ p['conv_out_w']

    rq_h = heads(rq, RET_HEADS)
    rk_h = heads(rk, RET_HEADS)
    if latent:
        rq_h, rk_h = rope_2d(rq_h), rope_2d(rk_h)
    rk_h = rk_h * RET_HD ** -0.5
    rv_h = heads(rv, RET_HEADS)
    log_gamma = jax.nn.log_sigmoid(p['ret_decay'].astype(f32))
    y_f, sf = retention_chunked(rq_h, rk_h, rv_h, log_gamma[0], s0[:, 0])
    y_bw, sb = retention_chunked(_flip(rq_h), _flip(rk_h), _flip(rv_h), log_gamma[1], s0[:, 1])
    h_c = head_norm(y_f + _flip(y_bw), p['ret_norm_w']).astype(h.dtype) * jax.nn.silu(rg)
    y_c = h_c @ p['ret_out_w']

    g = jax.nn.sigmoid(gm).reshape(bsz, t, N_BRANCH, D_MODEL)
    merged = g[:, :, 0] * y_a + g[:, :, 1] * y_b + g[:, :, 2] * y_c
    out = merged @ p['out_w']
    new_state = (jnp.stack([cf, cb], 1), jnp.stack([nf, nb], 1), jnp.stack([mf, mb], 1), jnp.stack([sf, sb], 1))
    return out, new_state


def trunk_layer(x, mod, p, init, latent):
    shift1, scale1, gate1, shift2, scale2, gate2 = jnp.split(mod[:, None, :], 6, axis=-1)
    h = x * (1.0 + scale1) + shift1
    mix, new_state = mixing_sublayer(h, p, init, latent)
    x = layer_norm(DEEPNORM_ALPHA * x + gate1 * mix, p['ln1_w'], p['ln1_b'])
    h = x * (1.0 + scale2) + shift2
    a, gt = jnp.split(h @ p['ffn_w13'], 2, axis=-1)
    ff = (jax.nn.silu(gt) * a) @ p['ffn_w2']
    x = layer_norm(DEEPNORM_ALPHA * x + gate2 * ff, p['ln2_w'], p['ln2_b'])
    return x, new_state


def setup_inputs(seed: int = 0) -> dict:
    key = jax.random.key(seed)
    ks = iter(jax.random.split(key, 40))

    def nrm(shape, scale):
        return jax.random.normal(next(ks), shape, jnp.float32) * scale

    fl = np.linspace(3.0, 6.0, MLSTM_HEADS).astype(np.float32)
    zh = np.zeros((MLSTM_HEADS,), np.float32)
    gate_base = jnp.asarray(np.stack([zh, fl, zh, fl], 0))
    gam = np.log(2.0 ** (5.0 + np.arange(RET_HEADS)) - 1.0).astype(np.float32)
    dec_base = jnp.asarray(np.stack([gam, gam], 0))
    return {
        'x_prompt': nrm((BATCH, SEQ, D_MODEL), 1.0),
        'x_sample': nrm((DEC_BATCH, DEC_SEQ, D_MODEL), 1.0),
        'state_mlstm_C': nrm((DEC_BATCH, DEPTH, 2, MLSTM_HEADS, MLSTM_HD, MLSTM_HD), 0.3),
        'state_mlstm_n': nrm((DEC_BATCH, DEPTH, 2, MLSTM_HEADS, MLSTM_HD), 0.3),
        'state_mlstm_m': nrm((DEC_BATCH, DEPTH, 2, MLSTM_HEADS), 1.0),
        'state_ret_S': nrm((DEC_BATCH, DEPTH, 2, RET_HEADS, RET_HD, RET_HD), 0.5),
        'c': nrm((DEC_BATCH, D_MODEL), 1.0),
        'c_ctx': nrm((D_MODEL,), 1.0),
        'ada_w': nrm((DEPTH, D_MODEL, 6 * D_MODEL), 0.5 * D_MODEL ** -0.5),
        'ada_b': nrm((DEPTH, 6 * D_MODEL), 0.02),
        'in_w': nrm((DEPTH, D_MODEL, IN_WIDTH), D_MODEL ** -0.5),
        'mlstm_gate_b': gate_base[None] + nrm((DEPTH, 4, MLSTM_HEADS), 0.1),
        'mlstm_norm_w': 1.0 + nrm((DEPTH, MLSTM_W), 0.05),
        'mlstm_out_w': nrm((DEPTH, MLSTM_W, D_MODEL), MLSTM_W ** -0.5),
        'conv_w': nrm((DEPTH, CONV_K, CONV_W), CONV_K ** -0.5),
        'conv_b': nrm((DEPTH, CONV_W), 0.02),
        'conv_ln_w': 1.0 + nrm((DEPTH, CONV_W), 0.05),
        'conv_ln_b': nrm((DEPTH, CONV_W), 0.02),
        'conv_out_w': nrm((DEPTH, CONV_W, D_MODEL), CONV_W ** -0.5),
        'ret_decay': dec_base[None] + nrm((DEPTH, 2, RET_HEADS), 0.05),
        'ret_norm_w': 1.0 + nrm((DEPTH, RET_W), 0.05),
        'ret_out_w': nrm((DEPTH, RET_W, D_MODEL), RET_W ** -0.5),
        'out_w': nrm((DEPTH, D_MODEL, D_MODEL), DEEPNORM_BETA * D_MODEL ** -0.5),
        'ln1_w': 1.0 + nrm((DEPTH, D_MODEL), 0.05),
        'ln1_b': nrm((DEPTH, D_MODEL), 0.02),
        'ln2_w': 1.0 + nrm((DEPTH, D_MODEL), 0.05),
        'ln2_b': nrm((DEPTH, D_MODEL), 0.02),
        'ffn_w13': nrm((DEPTH, D_MODEL, 2 * FFN_HIDDEN), D_MODEL ** -0.5),
        'ffn_w2': nrm((DEPTH, FFN_HIDDEN, D_MODEL), DEEPNORM_BETA * FFN_HIDDEN ** -0.5),
    }


def reference(x_prompt, x_sample, state_mlstm_C, state_mlstm_n, state_mlstm_m, state_ret_S, c, c_ctx,
              ada_w, ada_b, in_w, mlstm_gate_b, mlstm_norm_w, mlstm_out_w, conv_w, conv_b, conv_ln_w, conv_ln_b,
              conv_out_w, ret_decay, ret_norm_w, ret_out_w, out_w, ln1_w, ln1_b, ln2_w, ln2_b, ffn_w13, ffn_w2):
    f32 = jnp.float32
    bsz = x_prompt.shape[0]
    zero_state = (jnp.zeros((bsz, 2, MLSTM_HEADS, MLSTM_HD, MLSTM_HD), f32),
                  jnp.zeros((bsz, 2, MLSTM_HEADS, MLSTM_HD), f32),
                  jnp.zeros((bsz, 2, MLSTM_HEADS), f32),
                  jnp.zeros((bsz, 2, RET_HEADS, RET_HD, RET_HD), f32))
    y_prompt, y_sample = x_prompt, x_sample
    new_c, new_n, new_m, new_s = [], [], [], []
    for l in range(DEPTH):
        p = {'in_w': in_w[l], 'mlstm_gate_b': mlstm_gate_b[l], 'mlstm_norm_w': mlstm_norm_w[l],
             'mlstm_out_w': mlstm_out_w[l], 'conv_w': conv_w[l], 'conv_b': conv_b[l], 'conv_ln_w': conv_ln_w[l],
             'conv_ln_b': conv_ln_b[l], 'conv_out_w': conv_out_w[l], 'ret_decay': ret_decay[l],
             'ret_norm_w': ret_norm_w[l], 'ret_out_w': ret_out_w[l], 'out_w': out_w[l], 'ln1_w': ln1_w[l],
             'ln1_b': ln1_b[l], 'ln2_w': ln2_w[l], 'ln2_b': ln2_b[l], 'ffn_w13': ffn_w13[l], 'ffn_w2': ffn_w2[l]}
        mod_ctx = jax.nn.silu(c_ctx)[None] @ ada_w[l] + ada_b[l]
        mod_lat = jax.nn.silu(c) @ ada_w[l] + ada_b[l]
        y_prompt, st = trunk_layer(y_prompt, mod_ctx, p, zero_state, False)
        new_c.append(st[0])
        new_n.append(st[1])
        new_m.append(st[2])
        new_s.append(st[3])
        cache = (state_mlstm_C[:, l], state_mlstm_n[:, l], state_mlstm_m[:, l], state_ret_S[:, l])
        y_sample, _ = trunk_layer(y_sample, mod_lat, p, cache, True)
    odt = x_prompt.dtype
    return (y_prompt, y_sample, jnp.stack(new_c, 1).astype(odt), jnp.stack(new_n, 1).astype(odt),
            jnp.stack(new_m, 1).astype(odt), jnp.stack(new_s, 1).astype(odt))
```

```python
import functools
import math

import jax
import jax.numpy as jnp
from jax import lax
from jax.experimental import pallas as pl
from jax.experimental.pallas import tpu as pltpu

F32 = jnp.float32
BF16 = jnp.bfloat16

D_MODEL = 1024
N_HEADS = 4
HEAD_DIM = 128
BRANCH_W = N_HEADS * HEAD_DIM
N_GATE_COLS = 4 * N_HEADS
GATE_LO = 4 * BRANCH_W
CONV_K = 31
CONV_PAD = CONV_K // 2
FFN_HIDDEN = 2816
GRID_W = 64
ROPE_BASE = 10000.0
LN_EPS = 1e-5
N_MIX = 10 * BRANCH_W
N_GM = 3 * D_MODEL
N_PACK = N_MIX + N_GM
LANES = 128

OFF_MQ, OFF_MK, OFF_MV, OFF_MO = 0, 512, 1024, 1536
OFF_CA, OFF_CG = 2048, 2560
OFF_RQ, OFF_RK, OFF_RV, OFF_RG = 3072, 3584, 4096, 4608

ROW_TILE = 512
SCAN_CHUNK = 256
CONV_ROWS = 128
FFN_TILE = 256


def _dot(a, b):
    return jnp.dot(a, b, preferred_element_type=F32)


def _dot_nt(a, b):
    return lax.dot_general(a, b, (((1,), (1,)), ((), ())), preferred_element_type=F32)


def _dot_tn(a, b):
    return lax.dot_general(a, b, (((0,), (0,)), ((), ())), preferred_element_type=F32)


def _layer_norm(z, w, b):
    mu = jnp.mean(z, axis=-1, keepdims=True)
    zc = z - mu
    var = jnp.mean(zc * zc, axis=-1, keepdims=True)
    return zc * lax.rsqrt(var + LN_EPS) * w + b


def _head_norm(y, w):
    mu = jnp.mean(y, axis=-1, keepdims=True)
    yc = y - mu
    var = jnp.mean(yc * yc, axis=-1, keepdims=True)
    return yc * lax.rsqrt(var + LN_EPS) * w


def _log_sigmoid(x):
    return jnp.minimum(x, 0.0) - jnp.log1p(jnp.exp(-jnp.abs(x)))


def _split3(x):
    hi = x.astype(BF16)
    r = x - hi.astype(F32)
    mid = r.astype(BF16)
    lo = (r - mid.astype(F32)).astype(BF16)
    return hi, mid, lo


def _tri_left(tri, x):
    hi, mid, lo = _split3(x)
    return _dot(tri, hi) + _dot(tri, mid) + _dot(tri, lo)


def _tri_right(x, tri):
    hi, mid, lo = _split3(x)
    return _dot(hi, tri) + _dot(mid, tri) + _dot(lo, tri)


def _ada_kernel(c_ref, w_ref, b_ref, o_ref):
    c = c_ref[...]
    s = c * jax.nn.sigmoid(c)
    o_ref[...] = _dot(s.astype(BF16), w_ref[...].astype(BF16)) + b_ref[...]


def _ada_mod(c_all, ada_w, ada_b):
    depth, _, n = ada_w.shape
    tn = 1024
    return pl.pallas_call(
        _ada_kernel,
        grid=(depth, n // tn),
        in_specs=[
            pl.BlockSpec((8, D_MODEL), lambda l, j: (0, 0)),
            pl.BlockSpec((None, D_MODEL, tn), lambda l, j: (l, 0, j)),
            pl.BlockSpec((None, 1, tn), lambda l, j: (l, 0, j)),
        ],
        out_specs=pl.BlockSpec((None, 8, tn), lambda l, j: (l, 0, j)),
        out_shape=jax.ShapeDtypeStruct((depth, 8, n), F32),
        name="ada_mod",
    )(c_all, ada_w, ada_b.reshape(depth, 1, n))


def _inproj_kernel(x_ref, mod_ref, w_ref, wg_ref, wgt_ref, p_ref, gc_ref, gr_ref, h_scr, *, ch):
    @pl.when(pl.program_id(1) == 0)
    def _():
        shift = mod_ref[0:1, :]
        scale = mod_ref[1:2, :]
        h = (x_ref[...] * (1.0 + scale) + shift).astype(BF16)
        h_scr[...] = h
        gc_ref[...] = _dot(h, wg_ref[...])
        for cc in range(h.shape[0] // ch):
            gr_ref[cc] = _dot_nt(wgt_ref[...], h[cc * ch:(cc + 1) * ch])

    p_ref[...] = _dot(h_scr[...], w_ref[...]).astype(BF16)


def _in_proj(x2d, mod, seq_len, layer, wp, wg, wgt):
    rows = x2d.shape[0]
    tm, tn, ch = ROW_TILE, 1024, SCAN_CHUNK
    n_seq = mod.shape[0]

    def seq_of(i):
        return (i * tm) // seq_len if n_seq > 1 else 0

    return pl.pallas_call(
        functools.partial(_inproj_kernel, ch=ch),
        grid=(rows // tm, N_PACK // tn),
        in_specs=[
            pl.BlockSpec((tm, D_MODEL), lambda i, j: (i, 0)),
            pl.BlockSpec((None, 6, D_MODEL), lambda i, j: (seq_of(i), 0, 0)),
            pl.BlockSpec((None, D_MODEL, tn), lambda i, j: (layer, 0, j)),
            pl.BlockSpec((None, D_MODEL, LANES), lambda i, j: (layer, 0, 0)),
            pl.BlockSpec((None, N_GATE_COLS, D_MODEL), lambda i, j: (layer, 0, 0)),
        ],
        out_specs=[
            pl.BlockSpec((tm, tn), lambda i, j: (i, j)),
            pl.BlockSpec((tm, LANES), lambda i, j: (i, 0)),
            pl.BlockSpec((tm // ch, N_GATE_COLS, ch), lambda i, j: (i, 0, 0)),
        ],
        out_shape=[
            jax.ShapeDtypeStruct((rows, N_PACK), BF16),
            jax.ShapeDtypeStruct((rows, LANES), F32),
            jax.ShapeDtypeStruct((rows // ch, N_GATE_COLS, ch), F32),
        ],
        scratch_shapes=[pltpu.VMEM((tm, D_MODEL), BF16)],
        compiler_params=pltpu.CompilerParams(dimension_semantics=("parallel", "arbitrary")),
        name="in_proj",
    )(x2d, mod, wp, wg, wgt)


def _seqmix_kernel(*refs, seq_len, ch, layer, latent, has_state):
    (p_ref, gc_ref, gr_ref, gbc_ref, gbr_ref, mnw_ref, cw_ref, cb_ref, clw_ref, clb_ref,
     dec_ref, rnw_ref) = refs[:12]
    pos = 12
    if has_state:
        c0_ref, n0_ref, m0_ref, s0_ref = refs[pos:pos + 4]
        pos += 4
    hm_ref = refs[pos]
    pos += 1
    if not has_state:
        cn_ref, nn_ref, mn_ref, sn_ref = refs[pos:pos + 4]
        pos += 4
    acc_a, acc_c, upad, rqs, rks, cst, nst, mst, sst, dsc = refs[pos:]

    n_ch = seq_len // ch
    hd = HEAD_DIM
    k_scale = HEAD_DIM ** -0.5
    b_idx = pl.program_id(0)

    row_i = lax.broadcasted_iota(jnp.int32, (ch, ch), 0)
    col_i = lax.broadcasted_iota(jnp.int32, (ch, ch), 1)
    lower = col_i <= row_i
    upper = col_i >= row_i
    tri_l = lower.astype(BF16)
    tri_u = upper.astype(BF16)
    pos_col = lax.broadcasted_iota(jnp.int32, (ch, 1), 0).astype(F32)

    zeros_pad = jnp.zeros((16, BRANCH_W), F32)
    upad[0:16, :] = zeros_pad
    upad[seq_len + 16:seq_len + 32, :] = zeros_pad
    for rb in range(seq_len // CONV_ROWS):
        r = slice(rb * CONV_ROWS, (rb + 1) * CONV_ROWS)
        ca = p_ref[r, OFF_CA:OFF_CA + BRANCH_W].astype(F32)
        cg = p_ref[r, OFF_CG:OFF_CG + BRANCH_W].astype(F32)
        upad[16 + rb * CONV_ROWS:16 + (rb + 1) * CONV_ROWS, :] = ca * jax.nn.sigmoid(cg)
    for rb in range(seq_len // CONV_ROWS):
        base = rb * CONV_ROWS + 16 - CONV_PAD
        acc = upad[base:base + CONV_ROWS, :] * cw_ref[0:1, :]
        for k in range(1, CONV_K):
            acc = acc + upad[base + k:base + k + CONV_ROWS, :] * cw_ref[k:k + 1, :]
        acc = acc + cb_ref[...]
        u = _layer_norm(acc, clw_ref[...], clb_ref[...])
        u = u * jax.nn.sigmoid(u)
        hm_ref[rb * CONV_ROWS:(rb + 1) * CONV_ROWS, BRANCH_W:2 * BRANCH_W] = u.astype(BF16)

    if latent:
        t_i = lax.broadcasted_iota(jnp.int32, (seq_len, hd), 0)
        lane = lax.broadcasted_iota(jnp.int32, (seq_len, hd), 1)
        pair = lane >> 1
        n_pairs = hd // 4
        freq = jnp.exp((pair & (n_pairs - 1)).astype(F32) * (-math.log(ROPE_BASE) / n_pairs))
        grid_pos = jnp.where(pair < n_pairs, t_i // GRID_W, t_i % GRID_W).astype(F32)
        ang = grid_pos * freq
        cos = jnp.cos(ang)
        sin = jnp.sin(ang)
        even = (lane & 1) == 0
        sin_signed = jnp.where(even, -sin, sin)

        def rope(x):
            swapped = jnp.where(even, pltpu.roll(x, hd - 1, 1), pltpu.roll(x, 1, 1))
            return x * cos + swapped * sin_signed
    else:
        def rope(x):
            return x

    for h in range(N_HEADS):
        hc = slice(h * hd, (h + 1) * hd)
        rqs[:, hc] = rope(p_ref[:, OFF_RQ + h * hd:OFF_RQ + (h + 1) * hd].astype(F32)).astype(BF16)
        rk = rope(p_ref[:, OFF_RK + h * hd:OFF_RK + (h + 1) * hd].astype(F32))
        rks[:, hc] = (rk * k_scale).astype(BF16)

    diff = (row_i - col_i).astype(F32)
    log_gamma = []
    for d in range(2):
        per_head = []
        for h in range(N_HEADS):
            r = d * N_HEADS + h
            per_head.append((_log_sigmoid(dec_ref[r:r + 1, :]), _log_sigmoid(dec_ref[r:r + 1, 0:1])))
        log_gamma.append(per_head)
    for h in range(N_HEADS):
        lg_f = log_gamma[0][h][0]
        lg_b = log_gamma[1][h][0]
        dsc[h] = (jnp.where(lower, jnp.exp(lg_f * jnp.maximum(diff, 0.0)), 0.0)
                  + jnp.where(upper, jnp.exp(lg_b * jnp.maximum(-diff, 0.0)), 0.0))

    for d in range(2):
        mask = lower if d == 0 else upper
        tri_col = tri_l if d == 0 else tri_u
        tri_row = tri_u if d == 0 else tri_l
        ig_type, fg_type = 2 * d, 2 * d + 1
        order = range(n_ch) if d == 0 else range(n_ch - 1, -1, -1)

        if has_state:
            for h in range(N_HEADS):
                cst[h] = c0_ref[d, h]
                sst[h] = s0_ref[d, h]
                mst[h:h + 1, :] = jnp.full((1, LANES), m0_ref[b_idx, layer, d, h], F32)
            nst[0:N_HEADS, :] = n0_ref[d]
        else:
            mst[...] = jnp.zeros(mst.shape, F32)

        for step, c in enumerate(order):
            no_carry = (not has_state) and step == 0
            need_update = (step < n_ch - 1) or (not has_state)
            rows = slice(c * ch, (c + 1) * ch)

            gc = gc_ref[rows, :] + gbc_ref[...]
            gr = gr_ref[c] + gbr_ref[...]
            cum_c = _tri_left(tri_col, _log_sigmoid(gc))
            cum_r = _tri_right(_log_sigmoid(gr), tri_row)

            for h in range(N_HEADS):
                hc = slice(h * hd, (h + 1) * hd)
                ci = ig_type * N_HEADS + h
                cf = fg_type * N_HEADS + h

                ig_c = gc[:, ci:ci + 1]
                b_c = cum_c[:, cf:cf + 1]
                ig_r = gr[ci:ci + 1, :]
                b_r = cum_r[cf:cf + 1, :]
                q = p_ref[rows, OFF_MQ + h * hd:OFF_MQ + (h + 1) * hd]
                kf = p_ref[rows, OFF_MK + h * hd:OFF_MK + (h + 1) * hd].astype(F32) * k_scale
                v = p_ref[rows, OFF_MV + h * hd:OFF_MV + (h + 1) * hd]
                m0 = mst[h:h + 1, 0:1]

                qk = _dot_nt(q, kf.astype(BF16))
                log_d = jnp.where(mask, b_c + (ig_r - b_r), -jnp.inf)
                inter = b_c + m0
                m = jnp.maximum(inter, jnp.max(log_d, axis=1, keepdims=True))
                s = qk * jnp.exp(log_d - m)
                den = jnp.sum(s, axis=1, keepdims=True)
                num = _dot(s.astype(BF16), v)
                if not no_carry:
                    w_inter = jnp.exp(inter - m)
                    c0 = cst[h]
                    n0 = nst[h:h + 1, :]
                    num = num + _dot(q, c0.astype(BF16)) * w_inter
                    den = den + jnp.sum(q.astype(F32) * n0, axis=1, keepdims=True) * w_inter
                h_dir = num / jnp.maximum(jnp.abs(den), jnp.exp(-m))

                if need_update:
                    b_last = b_c[ch - 1:ch, :] if d == 0 else b_c[0:1, :]
                    log_w = b_last - b_c + ig_c
                    m_new = jnp.maximum(b_last + m0, jnp.max(log_w, axis=0, keepdims=True))
                    kw = kf * jnp.exp(log_w - m_new)
                    c_new = _dot_tn(kw.astype(BF16), v)
                    n_new = jnp.sum(kw, axis=0, keepdims=True)
                    if not no_carry:
                        carry = jnp.exp(b_last + m0 - m_new)
                        c_new = c_new + c0 * carry
                        n_new = n_new + n0 * carry
                    cst[h] = c_new
                    nst[h:h + 1, :] = n_new
                    mst[h:h + 1, :] = jnp.broadcast_to(m_new, (1, LANES))

                if d == 0:
                    acc_a[rows, hc] = h_dir
                else:
                    y = _head_norm(acc_a[rows, hc] + h_dir, mnw_ref[:, hc])
                    o = p_ref[rows, OFF_MO + h * hd:OFF_MO + (h + 1) * hd].astype(F32)
                    hm_ref[rows, hc] = (y * jax.nn.sigmoid(o)).astype(BF16)

                lg = log_gamma[d][h][1]
                rq = rqs[rows, hc]
                rk = rks[rows, hc]
                rv = p_ref[rows, OFF_RV + h * hd:OFF_RV + (h + 1) * hd]
                if d == 0:
                    scores = _dot_nt(rq, rk) * dsc[h]
                    y_dir = _dot(scores.astype(BF16), rv)
                    xi = jnp.exp(lg * (pos_col + 1.0))
                    zeta = jnp.exp(lg * ((ch - 1.0) - pos_col))
                else:
                    y_dir = None
                    xi = jnp.exp(lg * (ch - pos_col))
                    zeta = jnp.exp(lg * pos_col)
                if not no_carry:
                    s0 = sst[h]
                    y_int = _dot(rq, s0.astype(BF16)) * xi
                    y_dir = y_int if y_dir is None else y_dir + y_int
                if need_update:
                    s_new = _dot_tn((rk.astype(F32) * zeta).astype(BF16), rv)
                    if not no_carry:
                        s_new = s_new + s0 * jnp.exp(lg * float(ch))
                    sst[h] = s_new

                hcc = slice(2 * BRANCH_W + h * hd, 2 * BRANCH_W + (h + 1) * hd)
                if d == 0:
                    acc_c[rows, hc] = y_dir
                else:
                    y_sum = acc_c[rows, hc] if y_dir is None else acc_c[rows, hc] + y_dir
                    y = _head_norm(y_sum, rnw_ref[:, hc])
                    g = p_ref[rows, OFF_RG + h * hd:OFF_RG + (h + 1) * hd].astype(F32)
                    hm_ref[rows, hcc] = (y * (g * jax.nn.sigmoid(g))).astype(BF16)

        if not has_state:
            for h in range(N_HEADS):
                cn_ref[d, h] = cst[h]
                sn_ref[d, h] = sst[h]
            nn_ref[d * N_HEADS:(d + 1) * N_HEADS, :] = nst[0:N_HEADS, :]
            mn_ref[d * N_HEADS:(d + 1) * N_HEADS, :] = mst[0:N_HEADS, :]


def _seq_mix(p, gc, gr, seq_len, layer, latent, params, state):
    rows = p.shape[0]
    bsz = rows // seq_len
    ch = SCAN_CHUNK
    n_ch = seq_len // ch
    has_state = state is not None

    def full(shape):
        return pl.BlockSpec(shape, lambda b: (0,) * len(shape))

    in_specs = [
        pl.BlockSpec((seq_len, N_MIX), lambda b: (b, 0)),
        pl.BlockSpec((seq_len, LANES), lambda b: (b, 0)),
        pl.BlockSpec((n_ch, N_GATE_COLS, ch), lambda b: (b, 0, 0)),
        full((1, LANES)), full((N_GATE_COLS, 1)),
        full((1, BRANCH_W)), full((32, BRANCH_W)), full((1, BRANCH_W)), full((1, BRANCH_W)),
        full((1, BRANCH_W)), full((8, ch)), full((1, BRANCH_W)),
    ]
    args = [p, gc, gr, params["gate_b_col"], params["gate_b_row"], params["mlstm_norm_w"], params["conv_w"],
            params["conv_b"], params["conv_ln_w"], params["conv_ln_b"], params["ret_decay"], params["ret_norm_w"]]
    out_specs = [pl.BlockSpec((seq_len, 3 * BRANCH_W), lambda b: (b, 0))]
    out_shape = [jax.ShapeDtypeStruct((rows, 3 * BRANCH_W), BF16)]
    if has_state:
        c0, n0, m0, s0 = state
        in_specs += [
            pl.BlockSpec((None, None, 2, N_HEADS, HEAD_DIM, HEAD_DIM), lambda b: (b, layer, 0, 0, 0, 0)),
            pl.BlockSpec((None, None, 2, N_HEADS, HEAD_DIM), lambda b: (b, layer, 0, 0, 0)),
            pl.BlockSpec(memory_space=pltpu.SMEM),
            pl.BlockSpec((None, None, 2, N_HEADS, HEAD_DIM, HEAD_DIM), lambda b: (b, layer, 0, 0, 0, 0)),
        ]
        args += [c0, n0, m0, s0]
    else:
        mat = pl.BlockSpec((None, 2, N_HEADS, HEAD_DIM, HEAD_DIM), lambda b: (b, 0, 0, 0, 0))
        vec = pl.BlockSpec((None, 2 * N_HEADS, HEAD_DIM), lambda b: (b, 0, 0))
        out_specs += [mat, vec, vec, mat]
        out_shape += [
            jax.ShapeDtypeStruct((bsz, 2, N_HEADS, HEAD_DIM, HEAD_DIM), F32),
            jax.ShapeDtypeStruct((bsz, 2 * N_HEADS, HEAD_DIM), F32),
            jax.ShapeDtypeStruct((bsz, 2 * N_HEADS, LANES), F32),
            jax.ShapeDtypeStruct((bsz, 2, N_HEADS, HEAD_DIM, HEAD_DIM), F32),
        ]
    scratch = [
        pltpu.VMEM((seq_len, BRANCH_W), F32),
        pltpu.VMEM((seq_len, BRANCH_W), F32),
        pltpu.VMEM((seq_len + 32, BRANCH_W), F32),
        pltpu.VMEM((seq_len, BRANCH_W), BF16),
        pltpu.VMEM((seq_len, BRANCH_W), BF16),
        pltpu.VMEM((N_HEADS, HEAD_DIM, HEAD_DIM), F32),
        pltpu.VMEM((8, HEAD_DIM), F32),
        pltpu.VMEM((8, LANES), F32),
        pltpu.VMEM((N_HEADS, HEAD_DIM, HEAD_DIM), F32),
        pltpu.VMEM((N_HEADS, ch, ch), F32),
    ]
    return pl.pallas_call(
        functools.partial(_seqmix_kernel, seq_len=seq_len, ch=ch, layer=layer, latent=latent, has_state=has_state),
        grid=(bsz,),
        in_specs=in_specs,
        out_specs=out_specs,
        out_shape=out_shape,
        scratch_shapes=scratch,
        compiler_params=pltpu.CompilerParams(dimension_semantics=("parallel",)),
        name="seq_mix_latent" if latent else "seq_mix_context",
    )(*args)


def _post_kernel(hm_ref, g0_ref, g1_ref, g2_ref, x_ref, mod_ref, wa_ref, wb_ref, wc_ref, wo_ref,
                 l1w_ref, l1b_ref, w13_ref, w2_ref, l2w_ref, l2b_ref, o_ref, *, alpha):
    ya = _dot(hm_ref[:, 0:BRANCH_W], wa_ref[...])
    yb = _dot(hm_ref[:, BRANCH_W:2 * BRANCH_W], wb_ref[...])
    yc = _dot(hm_ref[:, 2 * BRANCH_W:3 * BRANCH_W], wc_ref[...])
    merged = (jax.nn.sigmoid(g0_ref[...].astype(F32)) * ya
              + jax.nn.sigmoid(g1_ref[...].astype(F32)) * yb
              + jax.nn.sigmoid(g2_ref[...].astype(F32)) * yc)
    mix = _dot(merged.astype(BF16), wo_ref[...])
    gate1 = mod_ref[2:3, :]
    shift2 = mod_ref[3:4, :]
    scale2 = mod_ref[4:5, :]
    gate2 = mod_ref[5:6, :]
    x1 = _layer_norm(alpha * x_ref[...] + gate1 * mix, l1w_ref[...], l1b_ref[...])
    h2 = (x1 * (1.0 + scale2) + shift2).astype(BF16)
    ff = jnp.zeros(x1.shape, F32)
    for c in range(FFN_HIDDEN // FFN_TILE):
        a = _dot(h2, w13_ref[:, c * FFN_TILE:(c + 1) * FFN_TILE])
        gt = _dot(h2, w13_ref[:, FFN_HIDDEN + c * FFN_TILE:FFN_HIDDEN + (c + 1) * FFN_TILE])
        act = (gt * jax.nn.sigmoid(gt)) * a
        ff = ff + _dot(act.astype(BF16), w2_ref[c * FFN_TILE:(c + 1) * FFN_TILE, :])
    o_ref[...] = _layer_norm(alpha * x1 + gate2 * ff, l2w_ref[...], l2b_ref[...])


def _post(hm, p, x2d, mod, seq_len, layer, w, alpha):
    rows = x2d.shape[0]
    tm = ROW_TILE
    n_seq = mod.shape[0]
    gm_blk = N_MIX // D_MODEL

    def seq_of(i):
        return (i * tm) // seq_len if n_seq > 1 else 0

    def weight(shape):
        nd = len(shape)
        return pl.BlockSpec((None,) + shape, lambda i: (layer,) + (0,) * nd, pipeline_mode=pl.Buffered(1))

    vec = weight((1, D_MODEL))
    return pl.pallas_call(
        functools.partial(_post_kernel, alpha=alpha),
        grid=(rows // tm,),
        in_specs=[
            pl.BlockSpec((tm, 3 * BRANCH_W), lambda i: (i, 0)),
            pl.BlockSpec((tm, D_MODEL), lambda i: (i, gm_blk)),
            pl.BlockSpec((tm, D_MODEL), lambda i: (i, gm_blk + 1)),
            pl.BlockSpec((tm, D_MODEL), lambda i: (i, gm_blk + 2)),
            pl.BlockSpec((tm, D_MODEL), lambda i: (i, 0)),
            pl.BlockSpec((None, 6, D_MODEL), lambda i: (seq_of(i), 0, 0)),
            weight((BRANCH_W, D_MODEL)), weight((BRANCH_W, D_MODEL)), weight((BRANCH_W, D_MODEL)),
            weight((D_MODEL, D_MODEL)), vec, vec,
            weight((D_MODEL, 2 * FFN_HIDDEN)), weight((FFN_HIDDEN, D_MODEL)), vec, vec,
        ],
        out_specs=pl.BlockSpec((tm, D_MODEL), lambda i: (i, 0)),
        out_shape=jax.ShapeDtypeStruct((rows, D_MODEL), F32),
        compiler_params=pltpu.CompilerParams(dimension_semantics=("parallel",)),
        name="post",
    )(hm, p, p, p, x2d, mod, w["mlstm_out_w"], w["conv_out_w"], w["ret_out_w"], w["out_w"],
      w["ln1_w"], w["ln1_b"], w["ffn_w13"], w["ffn_w2"], w["ln2_w"], w["ln2_b"])


def kernel(x_prompt, x_sample, state_mlstm_C, state_mlstm_n, state_mlstm_m, state_ret_S, c, c_ctx, ada_w, ada_b, in_w, mlstm_gate_b, mlstm_norm_w, mlstm_out_w, conv_w, conv_b, conv_ln_w, conv_ln_b, conv_out_w, ret_decay, ret_norm_w, ret_out_w, out_w, ln1_w, ln1_b, ln2_w, ln2_b, ffn_w13, ffn_w2):
    depth = in_w.shape[0]
    alpha = (2.0 * depth) ** 0.25
    bsz, seq, _ = x_prompt.shape
    dbsz, dseq, _ = x_sample.shape
    ch = SCAN_CHUNK

    wp = jnp.concatenate([in_w[:, :, :GATE_LO], in_w[:, :, GATE_LO + N_GATE_COLS:]], axis=2).astype(BF16)
    w_gate = in_w[:, :, GATE_LO:GATE_LO + N_GATE_COLS].astype(BF16)
    wg = jnp.pad(w_gate, ((0, 0), (0, 0), (0, LANES - N_GATE_COLS)))
    wgt = jnp.swapaxes(w_gate, 1, 2)
    dense = {
        "mlstm_out_w": mlstm_out_w.astype(BF16), "conv_out_w": conv_out_w.astype(BF16),
        "ret_out_w": ret_out_w.astype(BF16), "out_w": out_w.astype(BF16),
        "ffn_w13": ffn_w13.astype(BF16), "ffn_w2": ffn_w2.astype(BF16),
        "ln1_w": ln1_w.reshape(depth, 1, D_MODEL), "ln1_b": ln1_b.reshape(depth, 1, D_MODEL),
        "ln2_w": ln2_w.reshape(depth, 1, D_MODEL), "ln2_b": ln2_b.reshape(depth, 1, D_MODEL),
    }
    gate_b_flat = mlstm_gate_b.reshape(depth, N_GATE_COLS)

    c_all = jnp.concatenate([c_ctx[None, :], c, jnp.zeros((8 - 1 - dbsz, D_MODEL), F32)], axis=0)
    mod = _ada_mod(c_all, ada_w, ada_b)

    y_prompt = x_prompt.reshape(bsz * seq, D_MODEL)
    y_sample = x_sample.reshape(dbsz * dseq, D_MODEL)
    new_c, new_n, new_m, new_s = [], [], [], []
    for l in range(depth):
        mix_params = {
            "gate_b_col": jnp.pad(gate_b_flat[l][None, :], ((0, 0), (0, LANES - N_GATE_COLS))),
            "gate_b_row": gate_b_flat[l][:, None],
            "mlstm_norm_w": mlstm_norm_w[l][None, :],
            "conv_w": jnp.pad(conv_w[l], ((0, 32 - CONV_K), (0, 0))),
            "conv_b": conv_b[l][None, :],
            "conv_ln_w": conv_ln_w[l][None, :],
            "conv_ln_b": conv_ln_b[l][None, :],
            "ret_decay": jnp.broadcast_to(ret_decay[l].reshape(2 * N_HEADS, 1), (2 * N_HEADS, ch)),
            "ret_norm_w": ret_norm_w[l][None, :],
        }
        mod_ctx = mod[l, 0:1].reshape(1, 6, D_MODEL)
        mod_lat = mod[l, 1:1 + dbsz].reshape(dbsz, 6, D_MODEL)

        p, gc, gr = _in_proj(y_prompt, mod_ctx, seq, l, wp, wg, wgt)
        hm, cn, nn, mn, sn = _seq_mix(p, gc, gr, seq, l, False, mix_params, None)
        y_prompt = _post(hm, p, y_prompt, mod_ctx, seq, l, dense, alpha)
        new_c.append(cn)
        new_n.append(nn.reshape(bsz, 2, N_HEADS, HEAD_DIM))
        new_m.append(mn[:, :, 0].reshape(bsz, 2, N_HEADS))
        new_s.append(sn)

        p, gc, gr = _in_proj(y_sample, mod_lat, dseq, l, wp, wg, wgt)
        (hm,) = _seq_mix(p, gc, gr, dseq, l, True, mix_params,
                         (state_mlstm_C, state_mlstm_n, state_mlstm_m, state_ret_S))
        y_sample = _post(hm, p, y_sample, mod_lat, dseq, l, dense, alpha)

    return (y_prompt.reshape(bsz, seq, D_MODEL), y_sample.reshape(dbsz, dseq, D_MODEL),
            jnp.stack(new_c, 1), jnp.stack(new_n, 1), jnp.stack(new_m, 1), jnp.stack(new_s, 1))
```

```python
import functools
import math

import jax
import jax.numpy as jnp
from jax import lax
from jax.experimental import pallas as pl
from jax.experimental.pallas import tpu as pltpu

F32 = jnp.float32
BF16 = jnp.bfloat16

D_MODEL = 1024
N_HEADS = 4
HEAD_DIM = 128
BRANCH_W = N_HEADS * HEAD_DIM
N_GATE_COLS = 4 * N_HEADS
GATE_LO = 4 * BRANCH_W
CONV_K = 31
CONV_PAD = CONV_K // 2
FFN_HIDDEN = 2816
GRID_W = 64
ROPE_BASE = 10000.0
LN_EPS = 1e-5
N_MIX = 10 * BRANCH_W
N_GM = 3 * D_MODEL
N_PACK = N_MIX + N_GM
LANES = 128
SUBLANES = 8

OFF_MQ, OFF_MK, OFF_MV, OFF_MO = 0, 512, 1024, 1536
OFF_CA, OFF_CG = 2048, 2560
OFF_RQ, OFF_RK, OFF_RV, OFF_RG = 3072, 3584, 4096, 4608

ROW_TILE = 512
SCAN_CHUNK = 256
CONV_ROWS = 128
CONV_WIN = CONV_ROWS + 24
FFN_TILE = 256


def _dot(a, b):
    return jnp.dot(a, b, preferred_element_type=F32)


def _dot_nt(a, b):
    return lax.dot_general(a, b, (((1,), (1,)), ((), ())), preferred_element_type=F32)


def _dot_tn(a, b):
    return lax.dot_general(a, b, (((0,), (0,)), ((), ())), preferred_element_type=F32)


def _layer_norm(z, w, b):
    mu = jnp.mean(z, axis=-1, keepdims=True)
    zc = z - mu
    var = jnp.mean(zc * zc, axis=-1, keepdims=True)
    return zc * lax.rsqrt(var + LN_EPS) * w + b


def _head_norm(y, w):
    mu = jnp.mean(y, axis=-1, keepdims=True)
    yc = y - mu
    var = jnp.mean(yc * yc, axis=-1, keepdims=True)
    return yc * lax.rsqrt(var + LN_EPS) * w


def _log_sigmoid(x):
    return jnp.minimum(x, 0.0) - jnp.log1p(jnp.exp(-jnp.abs(x)))


def _split3(x):
    hi = x.astype(BF16)
    r = x - hi.astype(F32)
    mid = r.astype(BF16)
    lo = (r - mid.astype(F32)).astype(BF16)
    return hi, mid, lo


def _exact_right(x, sel):
    hi, mid, lo = _split3(x)
    return _dot(hi, sel) + _dot(mid, sel) + _dot(lo, sel)


def _exact_transpose(x, sel):
    hi, mid, lo = _split3(x)
    return _dot_tn(hi, sel) + _dot_tn(mid, sel) + _dot_tn(lo, sel)


def _ada_kernel(c_ref, w_ref, b_ref, o_ref):
    c = c_ref[...]
    s = c * jax.nn.sigmoid(c)
    o_ref[...] = _dot(s.astype(BF16), w_ref[...].astype(BF16)) + b_ref[...]


def _ada_mod(c_all, ada_w, ada_b):
    depth, _, n = ada_w.shape
    tn = 1024
    return pl.pallas_call(
        _ada_kernel,
        grid=(depth, n // tn),
        in_specs=[
            pl.BlockSpec((8, D_MODEL), lambda l, j: (0, 0)),
            pl.BlockSpec((None, D_MODEL, tn), lambda l, j: (l, 0, j)),
            pl.BlockSpec((None, 1, tn), lambda l, j: (l, 0, j)),
        ],
        out_specs=pl.BlockSpec((None, 8, tn), lambda l, j: (l, 0, j)),
        out_shape=jax.ShapeDtypeStruct((depth, 8, n), F32),
        name="ada_mod",
    )(c_all, ada_w, ada_b.reshape(depth, 1, n))


def _inproj_kernel(x_ref, mod_ref, w_ref, wgt_ref, p_ref, gr_ref, h_scr, *, ch):
    @pl.when(pl.program_id(1) == 0)
    def _():
        shift = mod_ref[0:1, :]
        scale = mod_ref[1:2, :]
        h = (x_ref[...] * (1.0 + scale) + shift).astype(BF16)
        h_scr[...] = h
        for cc in range(h.shape[0] // ch):
            gr_ref[cc] = _dot_nt(wgt_ref[...], h[cc * ch:(cc + 1) * ch])

    p_ref[...] = _dot(h_scr[...], w_ref[...]).astype(BF16)


def _in_proj(x2d, mod, seq_len, layer, wp, wgt):
    rows = x2d.shape[0]
    tm, tn, ch = ROW_TILE, 1024, SCAN_CHUNK
    n_seq = mod.shape[0]

    def seq_of(i):
        return (i * tm) // seq_len if n_seq > 1 else 0

    return pl.pallas_call(
        functools.partial(_inproj_kernel, ch=ch),
        grid=(rows // tm, N_PACK // tn),
        in_specs=[
            pl.BlockSpec((tm, D_MODEL), lambda i, j: (i, 0)),
            pl.BlockSpec((None, 6, D_MODEL), lambda i, j: (seq_of(i), 0, 0)),
            pl.BlockSpec((None, D_MODEL, tn), lambda i, j: (layer, 0, j)),
            pl.BlockSpec((None, N_GATE_COLS, D_MODEL), lambda i, j: (layer, 0, 0)),
        ],
        out_specs=[
            pl.BlockSpec((tm, tn), lambda i, j: (i, j)),
            pl.BlockSpec((tm // ch, N_GATE_COLS, ch), lambda i, j: (i, 0, 0)),
        ],
        out_shape=[
            jax.ShapeDtypeStruct((rows, N_PACK), BF16),
            jax.ShapeDtypeStruct((rows // ch, N_GATE_COLS, ch), F32),
        ],
        scratch_shapes=[pltpu.VMEM((tm, D_MODEL), BF16)],
        compiler_params=pltpu.CompilerParams(dimension_semantics=("parallel", "arbitrary")),
        name="in_proj",
    )(x2d, mod, wp, wgt)


def _seqmix_kernel(*refs, seq_len, ch, layer, latent, has_state):
    (p_ref, gr_ref, gbr_ref, mnw_ref, cw_ref, cb_ref, clw_ref, clb_ref, dec_ref, rnw_ref) = refs[:10]
    pos = 10
    if has_state:
        c0_ref, n0_ref, m0_ref, s0_ref = refs[pos:pos + 4]
        pos += 4
    hm_ref = refs[pos]
    pos += 1
    if not has_state:
        cn_ref, nn_ref, mn_ref, sn_ref = refs[pos:pos + 4]
        pos += 4
    acc_a, acc_c, upad, shf, rqs, rks, cst, nst, mst, sst, dsc = refs[pos:pos + 11]
    pos += 11
    if latent:
        cos_t, sin_t = refs[pos:pos + 2]

    n_ch = seq_len // ch
    hd = HEAD_DIM
    k_scale = HEAD_DIM ** -0.5
    b_idx = pl.program_id(0)

    row_i = lax.broadcasted_iota(jnp.int32, (ch, ch), 0)
    col_i = lax.broadcasted_iota(jnp.int32, (ch, ch), 1)
    lower = col_i <= row_i
    upper = col_i >= row_i
    tri_l = lower.astype(BF16)
    tri_u = upper.astype(BF16)
    pos_col = lax.broadcasted_iota(jnp.int32, (ch, 1), 0).astype(F32)
    gate_lane = lax.broadcasted_iota(jnp.int32, (N_GATE_COLS, ch), 1)
    ones_col = (lax.broadcasted_iota(jnp.int32, (ch, hd), 1) == 0).astype(BF16)
    n_stack = 3 * N_GATE_COLS
    eye_t = (lax.broadcasted_iota(jnp.int32, (n_stack, LANES), 0)
             == lax.broadcasted_iota(jnp.int32, (n_stack, LANES), 1)).astype(BF16)

    @pl.when(b_idx == 0)
    def _():
        diff = (row_i - col_i).astype(F32)
        for h in range(N_HEADS):
            lg_f = _log_sigmoid(dec_ref[h:h + 1, :])
            lg_b = _log_sigmoid(dec_ref[N_HEADS + h:N_HEADS + h + 1, :])
            dsc[h] = (jnp.where(lower, jnp.exp(lg_f * jnp.maximum(diff, 0.0)), 0.0)
                      + jnp.where(upper, jnp.exp(lg_b * jnp.maximum(-diff, 0.0)), 0.0))
        if latent:
            t_i = lax.broadcasted_iota(jnp.int32, (seq_len, hd), 0)
            lane = lax.broadcasted_iota(jnp.int32, (seq_len, hd), 1)
            pair = lane >> 1
            n_pairs = hd // 4
            freq = jnp.exp((pair & (n_pairs - 1)).astype(F32) * (-math.log(ROPE_BASE) / n_pairs))
            grid_pos = jnp.where(pair < n_pairs, t_i // GRID_W, t_i % GRID_W).astype(F32)
            ang = grid_pos * freq
            sin = jnp.sin(ang)
            cos_t[...] = jnp.cos(ang)
            sin_t[...] = jnp.where((lane & 1) == 0, -sin, sin)

    zeros_pad = jnp.zeros((16, BRANCH_W), F32)
    upad[0:16, :] = zeros_pad
    upad[seq_len + 16:seq_len + 32, :] = zeros_pad
    for rb in range(seq_len // CONV_ROWS):
        r = slice(rb * CONV_ROWS, (rb + 1) * CONV_ROWS)
        ca = p_ref[r, OFF_CA:OFF_CA + BRANCH_W].astype(F32)
        cg = p_ref[r, OFF_CG:OFF_CG + BRANCH_W].astype(F32)
        upad[16 + rb * CONV_ROWS:16 + (rb + 1) * CONV_ROWS, :] = ca * jax.nn.sigmoid(cg)
    for rb in range(seq_len // CONV_ROWS):
        base = rb * CONV_ROWS
        for sft in range(1, SUBLANES):
            shf[sft] = upad[base + sft:base + sft + CONV_WIN, :]
        acc = None
        for k in range(CONV_K):
            a8, sft = divmod(k + 16 - CONV_PAD, SUBLANES)
            if sft == 0:
                win = upad[base + a8 * SUBLANES:base + a8 * SUBLANES + CONV_ROWS, :]
            else:
                win = shf[sft, a8 * SUBLANES:a8 * SUBLANES + CONV_ROWS, :]
            term = win * cw_ref[k:k + 1, :]
            acc = term if acc is None else acc + term
        acc = acc + cb_ref[...]
        u = _layer_norm(acc, clw_ref[...], clb_ref[...])
        u = u * jax.nn.sigmoid(u)
        hm_ref[rb * CONV_ROWS:(rb + 1) * CONV_ROWS, BRANCH_W:2 * BRANCH_W] = u.astype(BF16)

    if latent:
        even = (lax.broadcasted_iota(jnp.int32, (seq_len, hd), 1) & 1) == 0

        def rope(x):
            swapped = jnp.where(even, pltpu.roll(x, hd - 1, 1), pltpu.roll(x, 1, 1))
            return x * cos_t[...] + swapped * sin_t[...]
    else:
        def rope(x):
            return x

    for h in range(N_HEADS):
        hc = slice(h * hd, (h + 1) * hd)
        rqs[:, hc] = rope(p_ref[:, OFF_RQ + h * hd:OFF_RQ + (h + 1) * hd].astype(F32)).astype(BF16)
        rk = rope(p_ref[:, OFF_RK + h * hd:OFF_RK + (h + 1) * hd].astype(F32))
        rks[:, hc] = (rk * k_scale).astype(BF16)

    mst[...] = jnp.zeros(mst.shape, F32)

    for d in range(2):
        mask = lower if d == 0 else upper
        tri_row = tri_u if d == 0 else tri_l
        r0 = 2 * N_HEADS * d
        edge = ch - 1 if d == 0 else 0
        order = range(n_ch) if d == 0 else range(n_ch - 1, -1, -1)
        log_gamma = [_log_sigmoid(dec_ref[d * N_HEADS + h:d * N_HEADS + h + 1, 0:1]) for h in range(N_HEADS)]

        if has_state:
            n_hi, n_mid, n_lo = _split3(n0_ref[...])
            sel_row = lax.broadcasted_iota(jnp.int32, (2 * N_HEADS, LANES), 0)
            sel_lane = lax.broadcasted_iota(jnp.int32, (2 * N_HEADS, LANES), 1)
            for h in range(N_HEADS):
                sel = ((sel_row == d * N_HEADS + h) & (sel_lane == 0)).astype(BF16)
                cst[h, :, 0:hd] = c0_ref[d, h]
                cst[h, :, hd:2 * hd] = _dot_tn(n_hi, sel) + _dot_tn(n_mid, sel) + _dot_tn(n_lo, sel)
                sst[h] = s0_ref[d, h]
                mst[r0 + h:r0 + h + 1, :] = jnp.full((1, LANES), m0_ref[b_idx, layer, d, h], F32)

        for step, c in enumerate(order):
            no_carry = (not has_state) and step == 0
            need_update = (step < n_ch - 1) or (not has_state)
            rows = slice(c * ch, (c + 1) * ch)

            g = gr_ref[c] + gbr_ref[...]
            cum = _exact_right(_log_sigmoid(g), tri_row)
            b_rows = pltpu.roll(cum, N_GATE_COLS - N_HEADS, 0)
            a_all = g - b_rows
            cm = a_all
            sft = 1
            while sft < ch:
                if d == 0:
                    moved = jnp.where(gate_lane >= sft, pltpu.roll(cm, sft, 1), -jnp.inf)
                else:
                    moved = jnp.where(gate_lane < ch - sft, pltpu.roll(cm, ch - sft, 1), -jnp.inf)
                cm = jnp.maximum(cm, moved)
                sft *= 2
            m0_all = mst[:, 0:1]
            big_m = jnp.maximum(cm, m0_all)
            m_true = b_rows + big_m
            cols = _exact_transpose(jnp.concatenate([a_all, big_m, m_true], axis=0), eye_t)
            m_last4 = big_m[r0:r0 + N_HEADS, edge:edge + 1]
            m_new4 = cum[r0 + N_HEADS:r0 + 2 * N_HEADS, edge:edge + 1] + m_last4
            carry4 = jnp.exp(m0_all[r0:r0 + N_HEADS, :] - m_last4)

            for h in range(N_HEADS):
                hc = slice(h * hd, (h + 1) * hd)
                gi = r0 + h

                a_r = a_all[gi:gi + 1, :]
                a_c = cols[:, gi:gi + 1]
                bm_c = cols[:, N_GATE_COLS + gi:N_GATE_COLS + gi + 1]
                mt_c = cols[:, 2 * N_GATE_COLS + gi:2 * N_GATE_COLS + gi + 1]
                q = p_ref[rows, OFF_MQ + h * hd:OFF_MQ + (h + 1) * hd]
                kf = p_ref[rows, OFF_MK + h * hd:OFF_MK + (h + 1) * hd].astype(F32) * k_scale
                v = p_ref[rows, OFF_MV + h * hd:OFF_MV + (h + 1) * hd]
                v_aug = jnp.concatenate([v, ones_col], axis=1)

                s = _dot_nt(q, kf.astype(BF16)) * jnp.exp(jnp.where(mask, a_r - bm_c, -jnp.inf))
                nd = _dot(s.astype(BF16), v_aug)
                if not no_carry:
                    c_aug = cst[h]
                    w_inter = jnp.exp(m0_all[gi:gi + 1, :] - bm_c)
                    nd = nd + _dot(q, c_aug.astype(BF16)) * w_inter
                h_dir = nd[:, 0:hd] / jnp.maximum(jnp.abs(nd[:, hd:hd + 1]), jnp.exp(-mt_c))

                if need_update:
                    kw = kf * jnp.exp(a_c - m_last4[h:h + 1, :])
                    c_new = _dot_tn(kw.astype(BF16), v_aug)
                    if not no_carry:
                        c_new = c_new + c_aug * carry4[h:h + 1, :]
                    cst[h] = c_new
                    if not has_state:
                        n_new = jnp.sum(kw, axis=0, keepdims=True)
                        if not no_carry:
                            n_new = n_new + nst[gi:gi + 1, :] * carry4[h:h + 1, :]
                        nst[gi:gi + 1, :] = n_new

                if d == 0:
                    acc_a[rows, hc] = h_dir
                else:
                    y = _head_norm(acc_a[rows, hc] + h_dir, mnw_ref[:, hc])
                    o = p_ref[rows, OFF_MO + h * hd:OFF_MO + (h + 1) * hd].astype(F32)
                    hm_ref[rows, hc] = (y * jax.nn.sigmoid(o)).astype(BF16)

                lg = log_gamma[h]
                rq = rqs[rows, hc]
                rk = rks[rows, hc]
                rv = p_ref[rows, OFF_RV + h * hd:OFF_RV + (h + 1) * hd]
                if d == 0:
                    scores = _dot_nt(rq, rk) * dsc[h]
                    y_dir = _dot(scores.astype(BF16), rv)
                    xi = jnp.exp(lg * (pos_col + 1.0))
                    zeta = jnp.exp(lg * ((ch - 1.0) - pos_col))
                else:
                    y_dir = None
                    xi = jnp.exp(lg * (ch - pos_col))
                    zeta = jnp.exp(lg * pos_col)
                if not no_carry:
                    s0 = sst[h]
                    y_int = _dot(rq, s0.astype(BF16)) * xi
                    y_dir = y_int if y_dir is None else y_dir + y_int
                if need_update:
                    s_new = _dot_tn((rk.astype(F32) * zeta).astype(BF16), rv)
                    if not no_carry:
                        s_new = s_new + s0 * jnp.exp(lg * float(ch))
                    sst[h] = s_new

                hcc = slice(2 * BRANCH_W + h * hd, 2 * BRANCH_W + (h + 1) * hd)
                if d == 0:
                    acc_c[rows, hc] = y_dir
                else:
                    y_sum = acc_c[rows, hc] if y_dir is None else acc_c[rows, hc] + y_dir
                    y = _head_norm(y_sum, rnw_ref[:, hc])
                    gt = p_ref[rows, OFF_RG + h * hd:OFF_RG + (h + 1) * hd].astype(F32)
                    hm_ref[rows, hcc] = (y * (gt * jax.nn.sigmoid(gt))).astype(BF16)

            if need_update:
                mst[r0:r0 + N_HEADS, :] = jnp.broadcast_to(m_new4, (N_HEADS, LANES))

        if not has_state:
            for h in range(N_HEADS):
                cn_ref[d, h] = cst[h, :, 0:hd]
                sn_ref[d, h] = sst[h]
            nn_ref[d * N_HEADS:(d + 1) * N_HEADS, :] = nst[r0:r0 + N_HEADS, :]
            mn_ref[d * N_HEADS:(d + 1) * N_HEADS, :] = mst[r0:r0 + N_HEADS, :]


def _seq_mix(p, gr, seq_len, layer, latent, params, state):
    rows = p.shape[0]
    bsz = rows // seq_len
    ch = SCAN_CHUNK
    n_ch = seq_len // ch
    has_state = state is not None

    def full(shape):
        return pl.BlockSpec(shape, lambda b: (0,) * len(shape))

    in_specs = [
        pl.BlockSpec((seq_len, N_MIX), lambda b: (b, 0)),
        pl.BlockSpec((n_ch, N_GATE_COLS, ch), lambda b: (b, 0, 0)),
        full((N_GATE_COLS, 1)),
        full((1, BRANCH_W)), full((32, BRANCH_W)), full((1, BRANCH_W)), full((1, BRANCH_W)),
        full((1, BRANCH_W)), full((8, ch)), full((1, BRANCH_W)),
    ]
    args = [p, gr, params["gate_b_row"], params["mlstm_norm_w"], params["conv_w"],
            params["conv_b"], params["conv_ln_w"], params["conv_ln_b"], params["ret_decay"], params["ret_norm_w"]]
    out_specs = [pl.BlockSpec((seq_len, 3 * BRANCH_W), lambda b: (b, 0))]
    out_shape = [jax.ShapeDtypeStruct((rows, 3 * BRANCH_W), BF16)]
    if has_state:
        c0, n0, m0, s0 = state
        in_specs += [
            pl.BlockSpec((None, None, 2, N_HEADS, HEAD_DIM, HEAD_DIM), lambda b: (b, layer, 0, 0, 0, 0)),
            pl.BlockSpec((None, None, 2 * N_HEADS, HEAD_DIM), lambda b: (b, layer, 0, 0)),
            pl.BlockSpec(memory_space=pltpu.SMEM),
            pl.BlockSpec((None, None, 2, N_HEADS, HEAD_DIM, HEAD_DIM), lambda b: (b, layer, 0, 0, 0, 0)),
        ]
        args += [c0, n0.reshape(n0.shape[0], n0.shape[1], 2 * N_HEADS, HEAD_DIM), m0, s0]
    else:
        mat = pl.BlockSpec((None, 2, N_HEADS, HEAD_DIM, HEAD_DIM), lambda b: (b, 0, 0, 0, 0))
        vec = pl.BlockSpec((None, 2 * N_HEADS, HEAD_DIM), lambda b: (b, 0, 0))
        out_specs += [mat, vec, vec, mat]
        out_shape += [
            jax.ShapeDtypeStruct((bsz, 2, N_HEADS, HEAD_DIM, HEAD_DIM), F32),
            jax.ShapeDtypeStruct((bsz, 2 * N_HEADS, HEAD_DIM), F32),
            jax.ShapeDtypeStruct((bsz, 2 * N_HEADS, LANES), F32),
            jax.ShapeDtypeStruct((bsz, 2, N_HEADS, HEAD_DIM, HEAD_DIM), F32),
        ]
    scratch = [
        pltpu.VMEM((seq_len, BRANCH_W), F32),
        pltpu.VMEM((seq_len, BRANCH_W), F32),
        pltpu.VMEM((seq_len + 32, BRANCH_W), F32),
        pltpu.VMEM((SUBLANES, CONV_WIN, BRANCH_W), F32),
        pltpu.VMEM((seq_len, BRANCH_W), BF16),
        pltpu.VMEM((seq_len, BRANCH_W), BF16),
        pltpu.VMEM((N_HEADS, HEAD_DIM, 2 * HEAD_DIM), F32),
        pltpu.VMEM((N_GATE_COLS, HEAD_DIM), F32),
        pltpu.VMEM((N_GATE_COLS, LANES), F32),
        pltpu.VMEM((N_HEADS, HEAD_DIM, HEAD_DIM), F32),
        pltpu.VMEM((N_HEADS, ch, ch), F32),
    ]
    if latent:
        scratch += [pltpu.VMEM((seq_len, HEAD_DIM), F32), pltpu.VMEM((seq_len, HEAD_DIM), F32)]
    return pl.pallas_call(
        functools.partial(_seqmix_kernel, seq_len=seq_len, ch=ch, layer=layer, latent=latent, has_state=has_state),
        grid=(bsz,),
        in_specs=in_specs,
        out_specs=out_specs,
        out_shape=out_shape,
        scratch_shapes=scratch,
        compiler_params=pltpu.CompilerParams(dimension_semantics=("arbitrary",)),
        name="seq_mix_latent" if latent else "seq_mix_context",
    )(*args)


def _post_kernel(hm_ref, g0_ref, g1_ref, g2_ref, x_ref, mod_ref, wa_ref, wb_ref, wc_ref, wo_ref,
                 l1w_ref, l1b_ref, w13_ref, w2_ref, l2w_ref, l2b_ref, o_ref, *, alpha):
    ya = _dot(hm_ref[:, 0:BRANCH_W], wa_ref[...])
    yb = _dot(hm_ref[:, BRANCH_W:2 * BRANCH_W], wb_ref[...])
    yc = _dot(hm_ref[:, 2 * BRANCH_W:3 * BRANCH_W], wc_ref[...])
    merged = (jax.nn.sigmoid(g0_ref[...].astype(F32)) * ya
              + jax.nn.sigmoid(g1_ref[...].astype(F32)) * yb
              + jax.nn.sigmoid(g2_ref[...].astype(F32)) * yc)
    mix = _dot(merged.astype(BF16), wo_ref[...])
    gate1 = mod_ref[2:3, :]
    shift2 = mod_ref[3:4, :]
    scale2 = mod_ref[4:5, :]
    gate2 = mod_ref[5:6, :]
    x1 = _layer_norm(alpha * x_ref[...] + gate1 * mix, l1w_ref[...], l1b_ref[...])
    h2 = (x1 * (1.0 + scale2) + shift2).astype(BF16)
    ff = jnp.zeros(x1.shape, F32)
    for c in range(FFN_HIDDEN // FFN_TILE):
        a = _dot(h2, w13_ref[:, c * FFN_TILE:(c + 1) * FFN_TILE])
        gt = _dot(h2, w13_ref[:, FFN_HIDDEN + c * FFN_TILE:FFN_HIDDEN + (c + 1) * FFN_TILE])
        act = (gt * jax.nn.sigmoid(gt)) * a
        ff = ff + _dot(act.astype(BF16), w2_ref[c * FFN_TILE:(c + 1) * FFN_TILE, :])
    o_ref[...] = _layer_norm(alpha * x1 + gate2 * ff, l2w_ref[...], l2b_ref[...])


def _post(hm, p, x2d, mod, seq_len, layer, w, alpha):
    rows = x2d.shape[0]
    tm = ROW_TILE
    n_seq = mod.shape[0]
    gm_blk = N_MIX // D_MODEL

    def seq_of(i):
        return (i * tm) // seq_len if n_seq > 1 else 0

    def weight(shape):
        nd = len(shape)
        return pl.BlockSpec((None,) + shape, lambda i: (layer,) + (0,) * nd, pipeline_mode=pl.Buffered(1))

    vec = weight((1, D_MODEL))
    return pl.pallas_call(
        functools.partial(_post_kernel, alpha=alpha),
        grid=(rows // tm,),
        in_specs=[
            pl.BlockSpec((tm, 3 * BRANCH_W), lambda i: (i, 0)),
            pl.BlockSpec((tm, D_MODEL), lambda i: (i, gm_blk)),
            pl.BlockSpec((tm, D_MODEL), lambda i: (i, gm_blk + 1)),
            pl.BlockSpec((tm, D_MODEL), lambda i: (i, gm_blk + 2)),
            pl.BlockSpec((tm, D_MODEL), lambda i: (i, 0)),
            pl.BlockSpec((None, 6, D_MODEL), lambda i: (seq_of(i), 0, 0)),
            weight((BRANCH_W, D_MODEL)), weight((BRANCH_W, D_MODEL)), weight((BRANCH_W, D_MODEL)),
            weight((D_MODEL, D_MODEL)), vec, vec,
            weight((D_MODEL, 2 * FFN_HIDDEN)), weight((FFN_HIDDEN, D_MODEL)), vec, vec,
        ],
        out_specs=pl.BlockSpec((tm, D_MODEL), lambda i: (i, 0)),
        out_shape=jax.ShapeDtypeStruct((rows, D_MODEL), F32),
        compiler_params=pltpu.CompilerParams(dimension_semantics=("parallel",)),
        name="post",
    )(hm, p, p, p, x2d, mod, w["mlstm_out_w"], w["conv_out_w"], w["ret_out_w"], w["out_w"],
      w["ln1_w"], w["ln1_b"], w["ffn_w13"], w["ffn_w2"], w["ln2_w"], w["ln2_b"])


def kernel(x_prompt, x_sample, state_mlstm_C, state_mlstm_n, state_mlstm_m, state_ret_S, c, c_ctx, ada_w, ada_b, in_w, mlstm_gate_b, mlstm_norm_w, mlstm_out_w, conv_w, conv_b, conv_ln_w, conv_ln_b, conv_out_w, ret_decay, ret_norm_w, ret_out_w, out_w, ln1_w, ln1_b, ln2_w, ln2_b, ffn_w13, ffn_w2):
    depth = in_w.shape[0]
    alpha = (2.0 * depth) ** 0.25
    bsz, seq, _ = x_prompt.shape
    dbsz, dseq, _ = x_sample.shape
    ch = SCAN_CHUNK

    wp = jnp.concatenate([in_w[:, :, :GATE_LO], in_w[:, :, GATE_LO + N_GATE_COLS:]], axis=2).astype(BF16)
    wgt = jnp.swapaxes(in_w[:, :, GATE_LO:GATE_LO + N_GATE_COLS].astype(BF16), 1, 2)
    dense = {
        "mlstm_out_w": mlstm_out_w.astype(BF16), "conv_out_w": conv_out_w.astype(BF16),
        "ret_out_w": ret_out_w.astype(BF16), "out_w": out_w.astype(BF16),
        "ffn_w13": ffn_w13.astype(BF16), "ffn_w2": ffn_w2.astype(BF16),
        "ln1_w": ln1_w.reshape(depth, 1, D_MODEL), "ln1_b": ln1_b.reshape(depth, 1, D_MODEL),
        "ln2_w": ln2_w.reshape(depth, 1, D_MODEL), "ln2_b": ln2_b.reshape(depth, 1, D_MODEL),
    }
    gate_b_flat = mlstm_gate_b.reshape(depth, N_GATE_COLS)

    c_all = jnp.concatenate([c_ctx[None, :], c, jnp.zeros((8 - 1 - dbsz, D_MODEL), F32)], axis=0)
    mod = _ada_mod(c_all, ada_w, ada_b)

    y_prompt = x_prompt.reshape(bsz * seq, D_MODEL)
    y_sample = x_sample.reshape(dbsz * dseq, D_MODEL)
    new_c, new_n, new_m, new_s = [], [], [], []
    for l in range(depth):
        mix_params = {
            "gate_b_row": gate_b_flat[l][:, None],
            "mlstm_norm_w": mlstm_norm_w[l][None, :],
            "conv_w": jnp.pad(conv_w[l], ((0, 32 - CONV_K), (0, 0))),
            "conv_b": conv_b[l][None, :],
            "conv_ln_w": conv_ln_w[l][None, :],
            "conv_ln_b": conv_ln_b[l][None, :],
            "ret_decay": jnp.broadcast_to(ret_decay[l].reshape(2 * N_HEADS, 1), (2 * N_HEADS, ch)),
            "ret_norm_w": ret_norm_w[l][None, :],
        }
        mod_ctx = mod[l, 0:1].reshape(1, 6, D_MODEL)
        mod_lat = mod[l, 1:1 + dbsz].reshape(dbsz, 6, D_MODEL)

        p, gr = _in_proj(y_prompt, mod_ctx, seq, l, wp, wgt)
        hm, cn, nn, mn, sn = _seq_mix(p, gr, seq, l, False, mix_params, None)
        y_prompt = _post(hm, p, y_prompt, mod_ctx, seq, l, dense, alpha)
        new_c.append(cn)
        new_n.append(nn.reshape(bsz, 2, N_HEADS, HEAD_DIM))
        new_m.append(mn[:, :, 0].reshape(bsz, 2, N_HEADS))
        new_s.append(sn)

        p, gr = _in_proj(y_sample, mod_lat, dseq, l, wp, wgt)
        (hm,) = _seq_mix(p, gr, dseq, l, True, mix_params,
                         (state_mlstm_C, state_mlstm_n, state_mlstm_m, state_ret_S))
        y_sample = _post(hm, p, y_sample, mod_lat, dseq, l, dense, alpha)

    return (y_prompt.reshape(bsz, seq, D_MODEL), y_sample.reshape(dbsz, dseq, D_MODEL),
            jnp.stack(new_c, 1), jnp.stack(new_n, 1), jnp.stack(new_m, 1), jnp.stack(new_s, 1))
```

```python
import functools
import math

import jax
import jax.numpy as jnp
from jax import lax
from jax.experimental import pallas as pl
from jax.experimental.pallas import tpu as pltpu

F32 = jnp.float32
BF16 = jnp.bfloat16

D_MODEL = 1024
N_HEADS = 4
HEAD_DIM = 128
BRANCH_W = N_HEADS * HEAD_DIM
N_GATE_COLS = 4 * N_HEADS
GATE_LO = 4 * BRANCH_W
CONV_K = 31
CONV_PAD = CONV_K // 2
FFN_HIDDEN = 2816
GRID_W = 64
ROPE_BASE = 10000.0
LN_EPS = 1e-5
N_MIX = 10 * BRANCH_W
N_GM = 3 * D_MODEL
LANES = 128
SUBLANES = 8

OFF_MQ, OFF_MK, OFF_MV, OFF_MO = 0, 512, 1024, 1536
OFF_CA, OFF_CG = 2048, 2560
OFF_RQ, OFF_RK, OFF_RV, OFF_RG = 3072, 3584, 4096, 4608

ROW_TILE = 512
PROJ_TILE = 1024
SCAN_CHUNK = 256
CONV_ROWS = 128
CONV_WIN = CONV_ROWS + 24
FFN_TILE = 256


def _dot(a, b):
    return jnp.dot(a, b, preferred_element_type=F32)


def _dot_nt(a, b):
    return lax.dot_general(a, b, (((1,), (1,)), ((), ())), preferred_element_type=F32)


def _dot_tn(a, b):
    return lax.dot_general(a, b, (((0,), (0,)), ((), ())), preferred_element_type=F32)


def _layer_norm(z, w, b):
    mu = jnp.mean(z, axis=-1, keepdims=True)
    zc = z - mu
    var = jnp.mean(zc * zc, axis=-1, keepdims=True)
    return zc * lax.rsqrt(var + LN_EPS) * w + b


def _head_norm(y, w):
    mu = jnp.mean(y, axis=-1, keepdims=True)
    yc = y - mu
    var = jnp.mean(yc * yc, axis=-1, keepdims=True)
    return yc * lax.rsqrt(var + LN_EPS) * w


def _log_sigmoid(x):
    return jnp.minimum(x, 0.0) - jnp.log1p(jnp.exp(-jnp.abs(x)))


def _split3(x):
    hi = x.astype(BF16)
    r = x - hi.astype(F32)
    mid = r.astype(BF16)
    lo = (r - mid.astype(F32)).astype(BF16)
    return hi, mid, lo


def _exact_right(x, sel):
    hi, mid, lo = _split3(x)
    return _dot(hi, sel) + _dot(mid, sel) + _dot(lo, sel)


def _exact_transpose(x, sel):
    hi, mid, lo = _split3(x)
    return _dot_tn(hi, sel) + _dot_tn(mid, sel) + _dot_tn(lo, sel)


def _ada_kernel(c_ref, w_ref, b_ref, o_ref):
    c = c_ref[...]
    s = c * jax.nn.sigmoid(c)
    o_ref[...] = _dot(s.astype(BF16), w_ref[...].astype(BF16)) + b_ref[...]


def _ada_mod(c_all, ada_w, ada_b):
    depth, _, n = ada_w.shape
    tn = 1024
    return pl.pallas_call(
        _ada_kernel,
        grid=(depth, n // tn),
        in_specs=[
            pl.BlockSpec((8, D_MODEL), lambda l, j: (0, 0)),
            pl.BlockSpec((None, D_MODEL, tn), lambda l, j: (l, 0, j)),
            pl.BlockSpec((None, 1, tn), lambda l, j: (l, 0, j)),
        ],
        out_specs=pl.BlockSpec((None, 8, tn), lambda l, j: (l, 0, j)),
        out_shape=jax.ShapeDtypeStruct((depth, 8, n), F32),
        name="ada_mod",
    )(c_all, ada_w, ada_b.reshape(depth, 1, n))


def _project(x_ref, mod_ref, w_ref, wgt_ref, p_out, gr_out, *, ch):
    shift = mod_ref[0:1, :]
    scale = mod_ref[1:2, :]
    h = (x_ref[...] * (1.0 + scale) + shift).astype(BF16)
    for jc in range(N_MIX // PROJ_TILE):
        cs = slice(jc * PROJ_TILE, (jc + 1) * PROJ_TILE)
        p_out[:, cs] = _dot(h, w_ref[:, cs]).astype(BF16)
    for cc in range(h.shape[0] // ch):
        gr_out[cc] = _dot_nt(wgt_ref[...], h[cc * ch:(cc + 1) * ch])


def _inproj_kernel(x_ref, mod_ref, w_ref, wgt_ref, p_ref, gr_ref, *, ch):
    _project(x_ref, mod_ref, w_ref, wgt_ref, p_ref, gr_ref, ch=ch)


def _weight_spec(shape, layer):
    nd = len(shape)
    return pl.BlockSpec((None,) + shape, lambda *_: (layer,) + (0,) * nd, pipeline_mode=pl.Buffered(1))


def _in_proj(x2d, mod, seq_len, layer, wp, wgt):
    rows = x2d.shape[0]
    tm, ch = ROW_TILE, SCAN_CHUNK
    n_seq = mod.shape[0]

    def seq_of(i):
        return (i * tm) // seq_len if n_seq > 1 else 0

    return pl.pallas_call(
        functools.partial(_inproj_kernel, ch=ch),
        grid=(rows // tm,),
        in_specs=[
            pl.BlockSpec((tm, D_MODEL), lambda i: (i, 0)),
            pl.BlockSpec((None, 6, D_MODEL), lambda i: (seq_of(i), 0, 0)),
            _weight_spec((D_MODEL, N_MIX), layer),
            _weight_spec((N_GATE_COLS, D_MODEL), layer),
        ],
        out_specs=[
            pl.BlockSpec((tm, N_MIX), lambda i: (i, 0)),
            pl.BlockSpec((tm // ch, N_GATE_COLS, ch), lambda i: (i, 0, 0)),
        ],
        out_shape=[
            jax.ShapeDtypeStruct((rows, N_MIX), BF16),
            jax.ShapeDtypeStruct((rows // ch, N_GATE_COLS, ch), F32),
        ],
        compiler_params=pltpu.CompilerParams(dimension_semantics=("parallel",)),
        name="in_proj",
    )(x2d, mod, wp, wgt)


def _init_tables(dec_ref, dsc, rope_tabs, *, seq_len, ch):
    hd = HEAD_DIM
    row_i = lax.broadcasted_iota(jnp.int32, (ch, ch), 0)
    col_i = lax.broadcasted_iota(jnp.int32, (ch, ch), 1)
    diff = (row_i - col_i).astype(F32)
    for h in range(N_HEADS):
        lg_f = _log_sigmoid(dec_ref[h:h + 1, :])
        lg_b = _log_sigmoid(dec_ref[N_HEADS + h:N_HEADS + h + 1, :])
        dsc[h] = (jnp.where(col_i <= row_i, jnp.exp(lg_f * jnp.maximum(diff, 0.0)), 0.0)
                  + jnp.where(col_i >= row_i, jnp.exp(lg_b * jnp.maximum(-diff, 0.0)), 0.0))
    if rope_tabs is not None:
        cos_t, sin_t = rope_tabs
        t_i = lax.broadcasted_iota(jnp.int32, (seq_len, hd), 0)
        lane = lax.broadcasted_iota(jnp.int32, (seq_len, hd), 1)
        pair = lane >> 1
        n_pairs = hd // 4
        freq = jnp.exp((pair & (n_pairs - 1)).astype(F32) * (-math.log(ROPE_BASE) / n_pairs))
        grid_pos = jnp.where(pair < n_pairs, t_i // GRID_W, t_i % GRID_W).astype(F32)
        ang = grid_pos * freq
        sin = jnp.sin(ang)
        cos_t[...] = jnp.cos(ang)
        sin_t[...] = jnp.where((lane & 1) == 0, -sin, sin)


def _mix_sequence(p, gr, prm, state_in, hm_ref, state_out, scr, rope_tabs, *, seq_len, ch, layer, seq_idx):
    gbr_ref, mnw_ref, cw_ref, cb_ref, clw_ref, clb_ref, dec_ref, rnw_ref = prm
    acc_a, acc_c, upad, shf, rqs, rks, cst, nst, mst, sst, dsc = scr
    has_state = state_in is not None
    if has_state:
        c0_ref, n0_ref, m0_ref, s0_ref = state_in
    else:
        cn_ref, nn_ref, mn_ref, sn_ref = state_out

    n_ch = seq_len // ch
    hd = HEAD_DIM
    k_scale = HEAD_DIM ** -0.5

    row_i = lax.broadcasted_iota(jnp.int32, (ch, ch), 0)
    col_i = lax.broadcasted_iota(jnp.int32, (ch, ch), 1)
    lower = col_i <= row_i
    upper = col_i >= row_i
    tri_l = lower.astype(BF16)
    tri_u = upper.astype(BF16)
    pos_col = lax.broadcasted_iota(jnp.int32, (ch, 1), 0).astype(F32)
    gate_lane = lax.broadcasted_iota(jnp.int32, (N_GATE_COLS, ch), 1)
    ones_col = (lax.broadcasted_iota(jnp.int32, (ch, hd), 1) == 0).astype(BF16)
    n_stack = 3 * N_GATE_COLS
    eye_t = (lax.broadcasted_iota(jnp.int32, (n_stack, LANES), 0)
             == lax.broadcasted_iota(jnp.int32, (n_stack, LANES), 1)).astype(BF16)

    zeros_pad = jnp.zeros((16, BRANCH_W), F32)
    upad[0:16, :] = zeros_pad
    upad[seq_len + 16:seq_len + 32, :] = zeros_pad
    for rb in range(seq_len // CONV_ROWS):
        r = slice(rb * CONV_ROWS, (rb + 1) * CONV_ROWS)
        ca = p[r, OFF_CA:OFF_CA + BRANCH_W].astype(F32)
        cg = p[r, OFF_CG:OFF_CG + BRANCH_W].astype(F32)
        upad[16 + rb * CONV_ROWS:16 + (rb + 1) * CONV_ROWS, :] = ca * jax.nn.sigmoid(cg)
    for rb in range(seq_len // CONV_ROWS):
        base = rb * CONV_ROWS
        for sft in range(1, SUBLANES):
            shf[sft] = upad[base + sft:base + sft + CONV_WIN, :]
        acc = None
        for k in range(CONV_K):
            a8, sft = divmod(k + 16 - CONV_PAD, SUBLANES)
            if sft == 0:
                win = upad[base + a8 * SUBLANES:base + a8 * SUBLANES + CONV_ROWS, :]
            else:
                win = shf[sft, a8 * SUBLANES:a8 * SUBLANES + CONV_ROWS, :]
            term = win * cw_ref[k:k + 1, :]
            acc = term if acc is None else acc + term
        acc = acc + cb_ref[...]
        u = _layer_norm(acc, clw_ref[...], clb_ref[...])
        u = u * jax.nn.sigmoid(u)
        hm_ref[rb * CONV_ROWS:(rb + 1) * CONV_ROWS, BRANCH_W:2 * BRANCH_W] = u.astype(BF16)

    if rope_tabs is not None:
        cos_t, sin_t = rope_tabs
        even = (lax.broadcasted_iota(jnp.int32, (seq_len, hd), 1) & 1) == 0

        def rope(x):
            swapped = jnp.where(even, pltpu.roll(x, hd - 1, 1), pltpu.roll(x, 1, 1))
            return x * cos_t[...] + swapped * sin_t[...]
    else:
        def rope(x):
            return x

    for h in range(N_HEADS):
        hc = slice(h * hd, (h + 1) * hd)
        rqs[:, hc] = rope(p[:, OFF_RQ + h * hd:OFF_RQ + (h + 1) * hd].astype(F32)).astype(BF16)
        rk = rope(p[:, OFF_RK + h * hd:OFF_RK + (h + 1) * hd].astype(F32))
        rks[:, hc] = (rk * k_scale).astype(BF16)

    mst[...] = jnp.zeros(mst.shape, F32)

    for d in range(2):
        mask = lower if d == 0 else upper
        tri_row = tri_u if d == 0 else tri_l
        r0 = 2 * N_HEADS * d
        edge = ch - 1 if d == 0 else 0
        order = range(n_ch) if d == 0 else range(n_ch - 1, -1, -1)
        log_gamma = [_log_sigmoid(dec_ref[d * N_HEADS + h:d * N_HEADS + h + 1, 0:1]) for h in range(N_HEADS)]

        if has_state:
            n_hi, n_mid, n_lo = _split3(n0_ref[...])
            sel_row = lax.broadcasted_iota(jnp.int32, (2 * N_HEADS, LANES), 0)
            sel_lane = lax.broadcasted_iota(jnp.int32, (2 * N_HEADS, LANES), 1)
            for h in range(N_HEADS):
                sel = ((sel_row == d * N_HEADS + h) & (sel_lane == 0)).astype(BF16)
                cst[h, :, 0:hd] = c0_ref[d, h]
                cst[h, :, hd:2 * hd] = _dot_tn(n_hi, sel) + _dot_tn(n_mid, sel) + _dot_tn(n_lo, sel)
                sst[h] = s0_ref[d, h]
                mst[r0 + h:r0 + h + 1, :] = jnp.full((1, LANES), m0_ref[seq_idx, layer, d, h], F32)

        for step, c in enumerate(order):
            no_carry = (not has_state) and step == 0
            need_update = (step < n_ch - 1) or (not has_state)
            rows = slice(c * ch, (c + 1) * ch)

            g = gr[c] + gbr_ref[...]
            cum = _exact_right(_log_sigmoid(g), tri_row)
            b_rows = pltpu.roll(cum, N_GATE_COLS - N_HEADS, 0)
            a_all = g - b_rows
            cm = a_all
            sft = 1
            while sft < ch:
                if d == 0:
                    moved = jnp.where(gate_lane >= sft, pltpu.roll(cm, sft, 1), -jnp.inf)
                else:
                    moved = jnp.where(gate_lane < ch - sft, pltpu.roll(cm, ch - sft, 1), -jnp.inf)
                cm = jnp.maximum(cm, moved)
                sft *= 2
            m0_all = mst[:, 0:1]
            big_m = jnp.maximum(cm, m0_all)
            m_true = b_rows + big_m
            cols = _exact_transpose(jnp.concatenate([a_all, big_m, m_true], axis=0), eye_t)
            m_last4 = big_m[r0:r0 + N_HEADS, edge:edge + 1]
            m_new4 = cum[r0 + N_HEADS:r0 + 2 * N_HEADS, edge:edge + 1] + m_last4
            carry4 = jnp.exp(m0_all[r0:r0 + N_HEADS, :] - m_last4)

            for h in range(N_HEADS):
                hc = slice(h * hd, (h + 1) * hd)
                gi = r0 + h

                a_r = a_all[gi:gi + 1, :]
                a_c = cols[:, gi:gi + 1]
                bm_c = cols[:, N_GATE_COLS + gi:N_GATE_COLS + gi + 1]
                mt_c = cols[:, 2 * N_GATE_COLS + gi:2 * N_GATE_COLS + gi + 1]
                q = p[rows, OFF_MQ + h * hd:OFF_MQ + (h + 1) * hd]
                kf = p[rows, OFF_MK + h * hd:OFF_MK + (h + 1) * hd].astype(F32) * k_scale
                v = p[rows, OFF_MV + h * hd:OFF_MV + (h + 1) * hd]
                v_aug = jnp.concatenate([v, ones_col], axis=1)

                s = _dot_nt(q, kf.astype(BF16)) * jnp.exp(jnp.where(mask, a_r - bm_c, -jnp.inf))
                nd = _dot(s.astype(BF16), v_aug)
                if not no_carry:
                    c_aug = cst[h]
                    w_inter = jnp.exp(m0_all[gi:gi + 1, :] - bm_c)
                    nd = nd + _dot(q, c_aug.astype(BF16)) * w_inter
                h_dir = nd[:, 0:hd] / jnp.maximum(jnp.abs(nd[:, hd:hd + 1]), jnp.exp(-mt_c))

                if need_update:
                    kw = kf * jnp.exp(a_c - m_last4[h:h + 1, :])
                    c_new = _dot_tn(kw.astype(BF16), v_aug)
                    if not no_carry:
                        c_new = c_new + c_aug * carry4[h:h + 1, :]
                    cst[h] = c_new
                    if not has_state:
                        n_new = jnp.sum(kw, axis=0, keepdims=True)
                        if not no_carry:
                            n_new = n_new + nst[gi:gi + 1, :] * carry4[h:h + 1, :]
                        nst[gi:gi + 1, :] = n_new

                if d == 0:
                    acc_a[rows, hc] = h_dir
                else:
                    y = _head_norm(acc_a[rows, hc] + h_dir, mnw_ref[:, hc])
                    o = p[rows, OFF_MO + h * hd:OFF_MO + (h + 1) * hd].astype(F32)
                    hm_ref[rows, hc] = (y * jax.nn.sigmoid(o)).astype(BF16)

                lg = log_gamma[h]
                rq = rqs[rows, hc]
                rk = rks[rows, hc]
                rv = p[rows, OFF_RV + h * hd:OFF_RV + (h + 1) * hd]
                if d == 0:
                    scores = _dot_nt(rq, rk) * dsc[h]
                    y_dir = _dot(scores.astype(BF16), rv)
                    xi = jnp.exp(lg * (pos_col + 1.0))
                    zeta = jnp.exp(lg * ((ch - 1.0) - pos_col))
                else:
                    y_dir = None
                    xi = jnp.exp(lg * (ch - pos_col))
                    zeta = jnp.exp(lg * pos_col)
                if not no_carry:
                    s0 = sst[h]
                    y_int = _dot(rq, s0.astype(BF16)) * xi
                    y_dir = y_int if y_dir is None else y_dir + y_int
                if need_update:
                    s_new = _dot_tn((rk.astype(F32) * zeta).astype(BF16), rv)
                    if not no_carry:
                        s_new = s_new + s0 * jnp.exp(lg * float(ch))
                    sst[h] = s_new

                hcc = slice(2 * BRANCH_W + h * hd, 2 * BRANCH_W + (h + 1) * hd)
                if d == 0:
                    acc_c[rows, hc] = y_dir
                else:
                    y_sum = acc_c[rows, hc] if y_dir is None else acc_c[rows, hc] + y_dir
                    y = _head_norm(y_sum, rnw_ref[:, hc])
                    gt = p[rows, OFF_RG + h * hd:OFF_RG + (h + 1) * hd].astype(F32)
                    hm_ref[rows, hcc] = (y * (gt * jax.nn.sigmoid(gt))).astype(BF16)

            if need_update:
                mst[r0:r0 + N_HEADS, :] = jnp.broadcast_to(m_new4, (N_HEADS, LANES))

        if not has_state:
            for h in range(N_HEADS):
                cn_ref[d, h] = cst[h, :, 0:hd]
                sn_ref[d, h] = sst[h]
            nn_ref[d * N_HEADS:(d + 1) * N_HEADS, :] = nst[r0:r0 + N_HEADS, :]
            mn_ref[d * N_HEADS:(d + 1) * N_HEADS, :] = mst[r0:r0 + N_HEADS, :]


def _mix_scratch(seq_len, ch):
    return [
        pltpu.VMEM((seq_len, BRANCH_W), F32),
        pltpu.VMEM((seq_len, BRANCH_W), F32),
        pltpu.VMEM((seq_len + 32, BRANCH_W), F32),
        pltpu.VMEM((SUBLANES, CONV_WIN, BRANCH_W), F32),
        pltpu.VMEM((seq_len, BRANCH_W), BF16),
        pltpu.VMEM((seq_len, BRANCH_W), BF16),
        pltpu.VMEM((N_HEADS, HEAD_DIM, 2 * HEAD_DIM), F32),
        pltpu.VMEM((N_GATE_COLS, HEAD_DIM), F32),
        pltpu.VMEM((N_GATE_COLS, LANES), F32),
        pltpu.VMEM((N_HEADS, HEAD_DIM, HEAD_DIM), F32),
        pltpu.VMEM((N_HEADS, ch, ch), F32),
    ]


N_MIX_PARAMS = 8
N_MIX_SCRATCH = 11


def _mix_param_specs(ch, index_map):
    def full(shape):
        return pl.BlockSpec(shape, lambda *a: (0,) * len(shape))

    del index_map
    return [full((N_GATE_COLS, 1)), full((1, BRANCH_W)), full((32, BRANCH_W)), full((1, BRANCH_W)),
            full((1, BRANCH_W)), full((1, BRANCH_W)), full((8, ch)), full((1, BRANCH_W))]


def _mix_param_args(params):
    return [params["gate_b_row"], params["mlstm_norm_w"], params["conv_w"], params["conv_b"],
            params["conv_ln_w"], params["conv_ln_b"], params["ret_decay"], params["ret_norm_w"]]


def _seqmix_latent_kernel(*refs, seq_len, ch, layer):
    p_ref, gr_ref = refs[:2]
    prm = refs[2:2 + N_MIX_PARAMS]
    pos = 2 + N_MIX_PARAMS
    state_in = refs[pos:pos + 4]
    hm_ref = refs[pos + 4]
    scr = refs[pos + 5:pos + 5 + N_MIX_SCRATCH]
    rope_tabs = refs[pos + 5 + N_MIX_SCRATCH:]
    b_idx = pl.program_id(0)

    @pl.when(b_idx == 0)
    def _():
        _init_tables(prm[6], scr[10], rope_tabs, seq_len=seq_len, ch=ch)

    _mix_sequence(p_ref, gr_ref, prm, state_in, hm_ref, None, scr, rope_tabs,
                  seq_len=seq_len, ch=ch, layer=layer, seq_idx=b_idx)


def _seq_mix_latent(p, gr, seq_len, layer, params, state):
    rows = p.shape[0]
    bsz = rows // seq_len
    ch = SCAN_CHUNK
    n_ch = seq_len // ch
    c0, n0, m0, s0 = state
    mat = pl.BlockSpec((None, None, 2, N_HEADS, HEAD_DIM, HEAD_DIM), lambda b: (b, layer, 0, 0, 0, 0))
    in_specs = (
        [pl.BlockSpec((seq_len, N_MIX), lambda b: (b, 0)),
         pl.BlockSpec((n_ch, N_GATE_COLS, ch), lambda b: (b, 0, 0))]
        + _mix_param_specs(ch, None)
        + [mat, pl.BlockSpec((None, None, 2 * N_HEADS, HEAD_DIM), lambda b: (b, layer, 0, 0)),
           pl.BlockSpec(memory_space=pltpu.SMEM), mat])
    args = [p, gr] + _mix_param_args(params) + [
        c0, n0.reshape(n0.shape[0], n0.shape[1], 2 * N_HEADS, HEAD_DIM), m0, s0]
    scratch = _mix_scratch(seq_len, ch) + [pltpu.VMEM((seq_len, HEAD_DIM), F32),
                                           pltpu.VMEM((seq_len, HEAD_DIM), F32)]
    return pl.pallas_call(
        functools.partial(_seqmix_latent_kernel, seq_len=seq_len, ch=ch, layer=layer),
        grid=(bsz,),
        in_specs=in_specs,
        out_specs=pl.BlockSpec((seq_len, 3 * BRANCH_W), lambda b: (b, 0)),
        out_shape=jax.ShapeDtypeStruct((rows, 3 * BRANCH_W), BF16),
        scratch_shapes=scratch,
        compiler_params=pltpu.CompilerParams(dimension_semantics=("arbitrary",)),
        name="seq_mix_latent",
    )(*args)


def _mixer_context_kernel(*refs, seq_len, ch, layer):
    x_ref, mod_ref, w_ref, wgt_ref = refs[:4]
    prm = refs[4:4 + N_MIX_PARAMS]
    pos = 4 + N_MIX_PARAMS
    hm_ref = refs[pos]
    state_out = refs[pos + 1:pos + 5]
    p_cur, p_next, gr_cur, gr_next = refs[pos + 5:pos + 9]
    scr = refs[pos + 9:pos + 9 + N_MIX_SCRATCH]
    s_idx = pl.program_id(0)

    @pl.when(s_idx == 0)
    def _():
        _init_tables(prm[6], scr[10], None, seq_len=seq_len, ch=ch)
        _project(x_ref, mod_ref, w_ref, wgt_ref, p_next, gr_next, ch=ch)

    @pl.when(s_idx > 0)
    def _():
        for jc in range(N_MIX // PROJ_TILE):
            cs = slice(jc * PROJ_TILE, (jc + 1) * PROJ_TILE)
            p_cur[:, cs] = p_next[:, cs]
        gr_cur[...] = gr_next[...]
        _project(x_ref, mod_ref, w_ref, wgt_ref, p_next, gr_next, ch=ch)
        _mix_sequence(p_cur, gr_cur, prm, None, hm_ref, state_out, scr, None,
                      seq_len=seq_len, ch=ch, layer=layer, seq_idx=s_idx - 1)


def _mixer_context(x2d, mod, seq_len, layer, wp, wgt, params):
    rows = x2d.shape[0]
    bsz = rows // seq_len
    ch = SCAN_CHUNK
    n_ch = seq_len // ch

    def nxt(s):
        return jnp.minimum(s, bsz - 1)

    def cur(s):
        return jnp.maximum(s - 1, 0)

    mat = pl.BlockSpec((None, 2, N_HEADS, HEAD_DIM, HEAD_DIM), lambda s: (cur(s), 0, 0, 0, 0))
    vec = pl.BlockSpec((None, 2 * N_HEADS, HEAD_DIM), lambda s: (cur(s), 0, 0))
    in_specs = (
        [pl.BlockSpec((seq_len, D_MODEL), lambda s: (nxt(s), 0)),
         pl.BlockSpec((None, 6, D_MODEL), lambda s: (0, 0, 0)),
         _weight_spec((D_MODEL, N_MIX), layer),
         _weight_spec((N_GATE_COLS, D_MODEL), layer)]
        + _mix_param_specs(ch, None))
    out_specs = [pl.BlockSpec((seq_len, 3 * BRANCH_W), lambda s: (cur(s), 0)), mat, vec, vec, mat]
    out_shape = [
        jax.ShapeDtypeStruct((rows, 3 * BRANCH_W), BF16),
        jax.ShapeDtypeStruct((bsz, 2, N_HEADS, HEAD_DIM, HEAD_DIM), F32),
        jax.ShapeDtypeStruct((bsz, 2 * N_HEADS, HEAD_DIM), F32),
        jax.ShapeDtypeStruct((bsz, 2 * N_HEADS, LANES), F32),
        jax.ShapeDtypeStruct((bsz, 2, N_HEADS, HEAD_DIM, HEAD_DIM), F32),
    ]
    scratch = [
        pltpu.VMEM((seq_len, N_MIX), BF16),
        pltpu.VMEM((seq_len, N_MIX), BF16),
        pltpu.VMEM((n_ch, N_GATE_COLS, ch), F32),
        pltpu.VMEM((n_ch, N_GATE_COLS, ch), F32),
    ] + _mix_scratch(seq_len, ch)
    return pl.pallas_call(
        functools.partial(_mixer_context_kernel, seq_len=seq_len, ch=ch, layer=layer),
        grid=(bsz + 1,),
        in_specs=in_specs,
        out_specs=out_specs,
        out_shape=out_shape,
        scratch_shapes=scratch,
        compiler_params=pltpu.CompilerParams(dimension_semantics=("arbitrary",)),
        name="mixer_context",
    )(x2d, mod, wp, wgt, *_mix_param_args(params))


def _post_kernel(hm_ref, x_ref, mod_ref, wgm_ref, wa_ref, wb_ref, wc_ref, wo_ref,
                 l1w_ref, l1b_ref, w13_ref, w2_ref, l2w_ref, l2b_ref, o_ref, *, alpha):
    shift1 = mod_ref[0:1, :]
    scale1 = mod_ref[1:2, :]
    gate1 = mod_ref[2:3, :]
    shift2 = mod_ref[3:4, :]
    scale2 = mod_ref[4:5, :]
    gate2 = mod_ref[5:6, :]
    x = x_ref[...]
    h1 = (x * (1.0 + scale1) + shift1).astype(BF16)
    merged = None
    for g, w_ref in enumerate((wa_ref, wb_ref, wc_ref)):
        gate = jax.nn.sigmoid(_dot(h1, wgm_ref[:, g * D_MODEL:(g + 1) * D_MODEL]))
        term = gate * _dot(hm_ref[:, g * BRANCH_W:(g + 1) * BRANCH_W], w_ref[...])
        merged = term if merged is None else merged + term
    mix = _dot(merged.astype(BF16), wo_ref[...])
    x1 = _layer_norm(alpha * x + gate1 * mix, l1w_ref[...], l1b_ref[...])
    h2 = (x1 * (1.0 + scale2) + shift2).astype(BF16)
    ff = jnp.zeros(x1.shape, F32)
    for c in range(FFN_HIDDEN // FFN_TILE):
        a = _dot(h2, w13_ref[:, c * FFN_TILE:(c + 1) * FFN_TILE])
        gt = _dot(h2, w13_ref[:, FFN_HIDDEN + c * FFN_TILE:FFN_HIDDEN + (c + 1) * FFN_TILE])
        act = (gt * jax.nn.sigmoid(gt)) * a
        ff = ff + _dot(act.astype(BF16), w2_ref[c * FFN_TILE:(c + 1) * FFN_TILE, :])
    o_ref[...] = _layer_norm(alpha * x1 + gate2 * ff, l2w_ref[...], l2b_ref[...])


def _post(hm, x2d, mod, seq_len, layer, w, alpha):
    rows = x2d.shape[0]
    tm = ROW_TILE
    n_seq = mod.shape[0]

    def seq_of(i):
        return (i * tm) // seq_len if n_seq > 1 else 0

    vec = _weight_spec((1, D_MODEL), layer)
    return pl.pallas_call(
        functools.partial(_post_kernel, alpha=alpha),
        grid=(rows // tm,),
        in_specs=[
            pl.BlockSpec((tm, 3 * BRANCH_W), lambda i: (i, 0)),
            pl.BlockSpec((tm, D_MODEL), lambda i: (i, 0)),
            pl.BlockSpec((None, 6, D_MODEL), lambda i: (seq_of(i), 0, 0)),
            _weight_spec((D_MODEL, N_GM), layer),
            _weight_spec((BRANCH_W, D_MODEL), layer), _weight_spec((BRANCH_W, D_MODEL), layer),
            _weight_spec((BRANCH_W, D_MODEL), layer),
            _weight_spec((D_MODEL, D_MODEL), layer), vec, vec,
            _weight_spec((D_MODEL, 2 * FFN_HIDDEN), layer), _weight_spec((FFN_HIDDEN, D_MODEL), layer), vec, vec,
        ],
        out_specs=pl.BlockSpec((tm, D_MODEL), lambda i: (i, 0)),
        out_shape=jax.ShapeDtypeStruct((rows, D_MODEL), F32),
        compiler_params=pltpu.CompilerParams(dimension_semantics=("parallel",)),
        name="post",
    )(hm, x2d, mod, w["gm_w"], w["mlstm_out_w"], w["conv_out_w"], w["ret_out_w"], w["out_w"],
      w["ln1_w"], w["ln1_b"], w["ffn_w13"], w["ffn_w2"], w["ln2_w"], w["ln2_b"])


def kernel(x_prompt, x_sample, state_mlstm_C, state_mlstm_n, state_mlstm_m, state_ret_S, c, c_ctx, ada_w, ada_b, in_w, mlstm_gate_b, mlstm_norm_w, mlstm_out_w, conv_w, conv_b, conv_ln_w, conv_ln_b, conv_out_w, ret_decay, ret_norm_w, ret_out_w, out_w, ln1_w, ln1_b, ln2_w, ln2_b, ffn_w13, ffn_w2):
    depth = in_w.shape[0]
    alpha = (2.0 * depth) ** 0.25
    bsz, seq, _ = x_prompt.shape
    dbsz, dseq, _ = x_sample.shape
    ch = SCAN_CHUNK
    gate_hi = GATE_LO + N_GATE_COLS

    wp = jnp.concatenate([in_w[:, :, :GATE_LO], in_w[:, :, gate_hi:gate_hi + N_MIX - GATE_LO]], axis=2).astype(BF16)
    wgt = jnp.swapaxes(in_w[:, :, GATE_LO:gate_hi].astype(BF16), 1, 2)
    dense = {
        "gm_w": in_w[:, :, gate_hi + N_MIX - GATE_LO:].astype(BF16),
        "mlstm_out_w": mlstm_out_w.astype(BF16), "conv_out_w": conv_out_w.astype(BF16),
        "ret_out_w": ret_out_w.astype(BF16), "out_w": out_w.astype(BF16),
        "ffn_w13": ffn_w13.astype(BF16), "ffn_w2": ffn_w2.astype(BF16),
        "ln1_w": ln1_w.reshape(depth, 1, D_MODEL), "ln1_b": ln1_b.reshape(depth, 1, D_MODEL),
        "ln2_w": ln2_w.reshape(depth, 1, D_MODEL), "ln2_b": ln2_b.reshape(depth, 1, D_MODEL),
    }
    gate_b_flat = mlstm_gate_b.reshape(depth, N_GATE_COLS)

    c_all = jnp.concatenate([c_ctx[None, :], c, jnp.zeros((8 - 1 - dbsz, D_MODEL), F32)], axis=0)
    mod = _ada_mod(c_all, ada_w, ada_b)

    y_prompt = x_prompt.reshape(bsz * seq, D_MODEL)
    y_sample = x_sample.reshape(dbsz * dseq, D_MODEL)
    new_c, new_n, new_m, new_s = [], [], [], []
    for l in range(depth):
        mix_params = {
            "gate_b_row": gate_b_flat[l][:, None],
            "mlstm_norm_w": mlstm_norm_w[l][None, :],
            "conv_w": jnp.pad(conv_w[l], ((0, 32 - CONV_K), (0, 0))),
            "conv_b": conv_b[l][None, :],
            "conv_ln_w": conv_ln_w[l][None, :],
            "conv_ln_b": conv_ln_b[l][None, :],
            "ret_decay": jnp.broadcast_to(ret_decay[l].reshape(2 * N_HEADS, 1), (2 * N_HEADS, ch)),
            "ret_norm_w": ret_norm_w[l][None, :],
        }
        mod_ctx = mod[l, 0:1].reshape(1, 6, D_MODEL)
        mod_lat = mod[l, 1:1 + dbsz].reshape(dbsz, 6, D_MODEL)

        hm, cn, nn, mn, sn = _mixer_context(y_prompt, mod_ctx, seq, l, wp, wgt, mix_params)
        y_prompt = _post(hm, y_prompt, mod_ctx, seq, l, dense, alpha)
        new_c.append(cn)
        new_n.append(nn.reshape(bsz, 2, N_HEADS, HEAD_DIM))
        new_m.append(mn[:, :, 0].reshape(bsz, 2, N_HEADS))
        new_s.append(sn)

        p, gr = _in_proj(y_sample, mod_lat, dseq, l, wp, wgt)
        hm = _seq_mix_latent(p, gr, dseq, l, mix_params,
                             (state_mlstm_C, state_mlstm_n, state_mlstm_m, state_ret_S))
        y_sample = _post(hm, y_sample, mod_lat, dseq, l, dense, alpha)

    return (y_prompt.reshape(bsz, seq, D_MODEL), y_sample.reshape(dbsz, dseq, D_MODEL),
            jnp.stack(new_c, 1), jnp.stack(new_n, 1), jnp.stack(new_m, 1), jnp.stack(new_s, 1))
```

```python
import functools
import math

import jax
import jax.numpy as jnp
from jax import lax
from jax.experimental import pallas as pl
from jax.experimental.pallas import tpu as pltpu

F32 = jnp.float32
BF16 = jnp.bfloat16

D_MODEL = 1024
N_HEADS = 4
HEAD_DIM = 128
BRANCH_W = N_HEADS * HEAD_DIM
N_GATE_COLS = 4 * N_HEADS
GATE_LO = 4 * BRANCH_W
CONV_K = 31
CONV_PAD = CONV_K // 2
FFN_HIDDEN = 2816
GRID_W = 64
ROPE_BASE = 10000.0
LN_EPS = 1e-5
N_MIX = 10 * BRANCH_W
N_GM = 3 * D_MODEL
LANES = 128
SUBLANES = 8

OFF_MQ, OFF_MK, OFF_MV, OFF_MO = 0, 512, 1024, 1536
OFF_CA, OFF_CG = 2048, 2560
OFF_RQ, OFF_RK, OFF_RV, OFF_RG = 3072, 3584, 4096, 4608

ROW_TILE = 512
PROJ_TILE = 1024
SCAN_CHUNK = 256
CONV_ROWS = 128
CONV_WIN = CONV_ROWS + 24
FFN_TILE = 256


def _dot(a, b):
    return jnp.dot(a, b, preferred_element_type=F32)


def _dot_nt(a, b):
    return lax.dot_general(a, b, (((1,), (1,)), ((), ())), preferred_element_type=F32)


def _dot_tn(a, b):
    return lax.dot_general(a, b, (((0,), (0,)), ((), ())), preferred_element_type=F32)


def _layer_norm(z, w, b):
    mu = jnp.mean(z, axis=-1, keepdims=True)
    zc = z - mu
    var = jnp.mean(zc * zc, axis=-1, keepdims=True)
    return zc * lax.rsqrt(var + LN_EPS) * w + b


def _head_norm(y, w):
    mu = jnp.mean(y, axis=-1, keepdims=True)
    yc = y - mu
    var = jnp.mean(yc * yc, axis=-1, keepdims=True)
    return yc * lax.rsqrt(var + LN_EPS) * w


def _log_sigmoid(x):
    return jnp.minimum(x, 0.0) - jnp.log1p(jnp.exp(-jnp.abs(x)))


def _split3(x):
    hi = x.astype(BF16)
    r = x - hi.astype(F32)
    mid = r.astype(BF16)
    lo = (r - mid.astype(F32)).astype(BF16)
    return hi, mid, lo


def _exact_right(x, sel):
    hi, mid, lo = _split3(x)
    return _dot(hi, sel) + _dot(mid, sel) + _dot(lo, sel)


def _exact_transpose(x, sel):
    hi, mid, lo = _split3(x)
    return _dot_tn(hi, sel) + _dot_tn(mid, sel) + _dot_tn(lo, sel)


def _ada_kernel(c_ref, w_ref, b_ref, o_ref):
    c = c_ref[...]
    s = c * jax.nn.sigmoid(c)
    o_ref[...] = _dot(s.astype(BF16), w_ref[...].astype(BF16)) + b_ref[...]


def _ada_mod(c_all, ada_w, ada_b):
    depth, _, n = ada_w.shape
    tn = 1024
    return pl.pallas_call(
        _ada_kernel,
        grid=(depth, n // tn),
        in_specs=[
            pl.BlockSpec((8, D_MODEL), lambda l, j: (0, 0)),
            pl.BlockSpec((None, D_MODEL, tn), lambda l, j: (l, 0, j)),
            pl.BlockSpec((None, 1, tn), lambda l, j: (l, 0, j)),
        ],
        out_specs=pl.BlockSpec((None, 8, tn), lambda l, j: (l, 0, j)),
        out_shape=jax.ShapeDtypeStruct((depth, 8, n), F32),
        name="ada_mod",
    )(c_all, ada_w, ada_b.reshape(depth, 1, n))


def _project_pieces(x_ref, mod_ref, w_ref, wgt_ref, p_out, gr_out, h_scr, *, ch):
    n_rows = x_ref.shape[0]

    def modulate():
        shift = mod_ref[0:1, :]
        scale = mod_ref[1:2, :]
        h_scr[...] = (x_ref[...] * (1.0 + scale) + shift).astype(BF16)
        for cc in range(n_rows // ch):
            gr_out[cc] = _dot_nt(wgt_ref[...], h_scr[cc * ch:(cc + 1) * ch, :])

    def column_tile(jc):
        cs = slice(jc * PROJ_TILE, (jc + 1) * PROJ_TILE)
        p_out[:, cs] = _dot(h_scr[...], w_ref[:, cs]).astype(BF16)

    return [modulate] + [functools.partial(column_tile, jc) for jc in range(N_MIX // PROJ_TILE)]


def _project(x_ref, mod_ref, w_ref, wgt_ref, p_out, gr_out, h_scr, *, ch):
    for piece in _project_pieces(x_ref, mod_ref, w_ref, wgt_ref, p_out, gr_out, h_scr, ch=ch):
        piece()


def _inproj_kernel(x_ref, mod_ref, w_ref, wgt_ref, p_ref, gr_ref, h_scr, *, ch):
    _project(x_ref, mod_ref, w_ref, wgt_ref, p_ref, gr_ref, h_scr, ch=ch)


def _weight_spec(shape, layer):
    nd = len(shape)
    return pl.BlockSpec((None,) + shape, lambda *_: (layer,) + (0,) * nd, pipeline_mode=pl.Buffered(1))


def _in_proj(x2d, mod, seq_len, layer, wp, wgt):
    rows = x2d.shape[0]
    tm, ch = ROW_TILE, SCAN_CHUNK
    n_seq = mod.shape[0]

    def seq_of(i):
        return (i * tm) // seq_len if n_seq > 1 else 0

    return pl.pallas_call(
        functools.partial(_inproj_kernel, ch=ch),
        grid=(rows // tm,),
        in_specs=[
            pl.BlockSpec((tm, D_MODEL), lambda i: (i, 0)),
            pl.BlockSpec((None, 6, D_MODEL), lambda i: (seq_of(i), 0, 0)),
            _weight_spec((D_MODEL, N_MIX), layer),
            _weight_spec((N_GATE_COLS, D_MODEL), layer),
        ],
        out_specs=[
            pl.BlockSpec((tm, N_MIX), lambda i: (i, 0)),
            pl.BlockSpec((tm // ch, N_GATE_COLS, ch), lambda i: (i, 0, 0)),
        ],
        out_shape=[
            jax.ShapeDtypeStruct((rows, N_MIX), BF16),
            jax.ShapeDtypeStruct((rows // ch, N_GATE_COLS, ch), F32),
        ],
        scratch_shapes=[pltpu.VMEM((tm, D_MODEL), BF16)],
        compiler_params=pltpu.CompilerParams(dimension_semantics=("parallel",)),
        name="in_proj",
    )(x2d, mod, wp, wgt)


def _init_tables(dec_ref, dsc, rope_tabs, *, seq_len, ch):
    hd = HEAD_DIM
    row_i = lax.broadcasted_iota(jnp.int32, (ch, ch), 0)
    col_i = lax.broadcasted_iota(jnp.int32, (ch, ch), 1)
    diff = (row_i - col_i).astype(F32)
    for h in range(N_HEADS):
        lg_f = _log_sigmoid(dec_ref[h:h + 1, :])
        lg_b = _log_sigmoid(dec_ref[N_HEADS + h:N_HEADS + h + 1, :])
        dsc[h] = (jnp.where(col_i <= row_i, jnp.exp(lg_f * jnp.maximum(diff, 0.0)), 0.0)
                  + jnp.where(col_i >= row_i, jnp.exp(lg_b * jnp.maximum(-diff, 0.0)), 0.0))
    if rope_tabs is not None:
        cos_t, sin_t = rope_tabs
        t_i = lax.broadcasted_iota(jnp.int32, (seq_len, hd), 0)
        lane = lax.broadcasted_iota(jnp.int32, (seq_len, hd), 1)
        pair = lane >> 1
        n_pairs = hd // 4
        freq = jnp.exp((pair & (n_pairs - 1)).astype(F32) * (-math.log(ROPE_BASE) / n_pairs))
        grid_pos = jnp.where(pair < n_pairs, t_i // GRID_W, t_i % GRID_W).astype(F32)
        ang = grid_pos * freq
        sin = jnp.sin(ang)
        cos_t[...] = jnp.cos(ang)
        sin_t[...] = jnp.where((lane & 1) == 0, -sin, sin)


def _mix_sequence(p, gr, prm, state_in, hm_ref, state_out, scr, rope_tabs, *, seq_len, ch, layer, seq_idx,
                  fillers=()):
    fillers = list(fillers)

    def emit_filler():
        if fillers:
            fillers.pop(0)()

    gbr_ref, mnw_ref, cw_ref, cb_ref, clw_ref, clb_ref, dec_ref, rnw_ref = prm
    acc_a, acc_c, upad, shf, rqs, rks, cst, nst, mst, sst, dsc = scr
    has_state = state_in is not None
    if has_state:
        c0_ref, n0_ref, m0_ref, s0_ref = state_in
    else:
        cn_ref, nn_ref, mn_ref, sn_ref = state_out

    n_ch = seq_len // ch
    hd = HEAD_DIM
    k_scale = HEAD_DIM ** -0.5

    row_i = lax.broadcasted_iota(jnp.int32, (ch, ch), 0)
    col_i = lax.broadcasted_iota(jnp.int32, (ch, ch), 1)
    lower = col_i <= row_i
    upper = col_i >= row_i
    tri_l = lower.astype(BF16)
    tri_u = upper.astype(BF16)
    pos_col = lax.broadcasted_iota(jnp.int32, (ch, 1), 0).astype(F32)
    gate_lane = lax.broadcasted_iota(jnp.int32, (N_GATE_COLS, ch), 1)
    ones_col = (lax.broadcasted_iota(jnp.int32, (ch, hd), 1) == 0).astype(BF16)
    n_stack = 3 * N_GATE_COLS
    eye_t = (lax.broadcasted_iota(jnp.int32, (n_stack, LANES), 0)
             == lax.broadcasted_iota(jnp.int32, (n_stack, LANES), 1)).astype(BF16)

    zeros_pad = jnp.zeros((16, BRANCH_W), F32)
    upad[0:16, :] = zeros_pad
    upad[seq_len + 16:seq_len + 32, :] = zeros_pad
    for rb in range(seq_len // CONV_ROWS):
        r = slice(rb * CONV_ROWS, (rb + 1) * CONV_ROWS)
        ca = p[r, OFF_CA:OFF_CA + BRANCH_W].astype(F32)
        cg = p[r, OFF_CG:OFF_CG + BRANCH_W].astype(F32)
        upad[16 + rb * CONV_ROWS:16 + (rb + 1) * CONV_ROWS, :] = ca * jax.nn.sigmoid(cg)
    for rb in range(seq_len // CONV_ROWS):
        emit_filler()
        base = rb * CONV_ROWS
        for sft in range(1, SUBLANES):
            shf[sft] = upad[base + sft:base + sft + CONV_WIN, :]
        acc = None
        for k in range(CONV_K):
            a8, sft = divmod(k + 16 - CONV_PAD, SUBLANES)
            if sft == 0:
                win = upad[base + a8 * SUBLANES:base + a8 * SUBLANES + CONV_ROWS, :]
            else:
                win = shf[sft, a8 * SUBLANES:a8 * SUBLANES + CONV_ROWS, :]
            term = win * cw_ref[k:k + 1, :]
            acc = term if acc is None else acc + term
        acc = acc + cb_ref[...]
        u = _layer_norm(acc, clw_ref[...], clb_ref[...])
        u = u * jax.nn.sigmoid(u)
        hm_ref[rb * CONV_ROWS:(rb + 1) * CONV_ROWS, BRANCH_W:2 * BRANCH_W] = u.astype(BF16)

    if rope_tabs is not None:
        cos_t, sin_t = rope_tabs
        even = (lax.broadcasted_iota(jnp.int32, (seq_len, hd), 1) & 1) == 0

        def rope(x):
            swapped = jnp.where(even, pltpu.roll(x, hd - 1, 1), pltpu.roll(x, 1, 1))
            return x * cos_t[...] + swapped * sin_t[...]
    else:
        def rope(x):
            return x

    for h in range(N_HEADS):
        hc = slice(h * hd, (h + 1) * hd)
        rqs[:, hc] = rope(p[:, OFF_RQ + h * hd:OFF_RQ + (h + 1) * hd].astype(F32)).astype(BF16)
        rk = rope(p[:, OFF_RK + h * hd:OFF_RK + (h + 1) * hd].astype(F32))
        rks[:, hc] = (rk * k_scale).astype(BF16)

    mst[...] = jnp.zeros(mst.shape, F32)

    for d in range(2):
        mask = lower if d == 0 else upper
        tri_row = tri_u if d == 0 else tri_l
        r0 = 2 * N_HEADS * d
        edge = ch - 1 if d == 0 else 0
        order = range(n_ch) if d == 0 else range(n_ch - 1, -1, -1)
        log_gamma = [_log_sigmoid(dec_ref[d * N_HEADS + h:d * N_HEADS + h + 1, 0:1]) for h in range(N_HEADS)]

        if has_state:
            n_hi, n_mid, n_lo = _split3(n0_ref[...])
            sel_row = lax.broadcasted_iota(jnp.int32, (2 * N_HEADS, LANES), 0)
            sel_lane = lax.broadcasted_iota(jnp.int32, (2 * N_HEADS, LANES), 1)
            for h in range(N_HEADS):
                sel = ((sel_row == d * N_HEADS + h) & (sel_lane == 0)).astype(BF16)
                cst[h, :, 0:hd] = c0_ref[d, h]
                cst[h, :, hd:2 * hd] = _dot_tn(n_hi, sel) + _dot_tn(n_mid, sel) + _dot_tn(n_lo, sel)
                sst[h] = s0_ref[d, h]
                mst[r0 + h:r0 + h + 1, :] = jnp.full((1, LANES), m0_ref[seq_idx, layer, d, h], F32)

        for step, c in enumerate(order):
            no_carry = (not has_state) and step == 0
            need_update = (step < n_ch - 1) or (not has_state)
            rows = slice(c * ch, (c + 1) * ch)

            g = gr[c] + gbr_ref[...]
            cum = _exact_right(_log_sigmoid(g), tri_row)
            b_rows = pltpu.roll(cum, N_GATE_COLS - N_HEADS, 0)
            a_all = g - b_rows
            cm = a_all
            sft = 1
            while sft < ch:
                if d == 0:
                    moved = jnp.where(gate_lane >= sft, pltpu.roll(cm, sft, 1), -jnp.inf)
                else:
                    moved = jnp.where(gate_lane < ch - sft, pltpu.roll(cm, ch - sft, 1), -jnp.inf)
                cm = jnp.maximum(cm, moved)
                sft *= 2
            m0_all = mst[:, 0:1]
            big_m = jnp.maximum(cm, m0_all)
            m_true = b_rows + big_m
            cols = _exact_transpose(jnp.concatenate([a_all, big_m, m_true], axis=0), eye_t)
            m_last4 = big_m[r0:r0 + N_HEADS, edge:edge + 1]
            m_new4 = cum[r0 + N_HEADS:r0 + 2 * N_HEADS, edge:edge + 1] + m_last4
            carry4 = jnp.exp(m0_all[r0:r0 + N_HEADS, :] - m_last4)

            for h in range(N_HEADS):
                emit_filler()
                hc = slice(h * hd, (h + 1) * hd)
                gi = r0 + h

                a_r = a_all[gi:gi + 1, :]
                a_c = cols[:, gi:gi + 1]
                bm_c = cols[:, N_GATE_COLS + gi:N_GATE_COLS + gi + 1]
                mt_c = cols[:, 2 * N_GATE_COLS + gi:2 * N_GATE_COLS + gi + 1]
                q = p[rows, OFF_MQ + h * hd:OFF_MQ + (h + 1) * hd]
                kf = p[rows, OFF_MK + h * hd:OFF_MK + (h + 1) * hd].astype(F32) * k_scale
                v = p[rows, OFF_MV + h * hd:OFF_MV + (h + 1) * hd]
                v_aug = jnp.concatenate([v, ones_col], axis=1)

                s = _dot_nt(q, kf.astype(BF16)) * jnp.exp(jnp.where(mask, a_r - bm_c, -jnp.inf))
                nd = _dot(s.astype(BF16), v_aug)
                if not no_carry:
                    c_aug = cst[h]
                    w_inter = jnp.exp(m0_all[gi:gi + 1, :] - bm_c)
                    nd = nd + _dot(q, c_aug.astype(BF16)) * w_inter
                h_dir = nd[:, 0:hd] / jnp.maximum(jnp.abs(nd[:, hd:hd + 1]), jnp.exp(-mt_c))

                if need_update:
                    kw = kf * jnp.exp(a_c - m_last4[h:h + 1, :])
                    c_new = _dot_tn(kw.astype(BF16), v_aug)
                    if not no_carry:
                        c_new = c_new + c_aug * carry4[h:h + 1, :]
                    cst[h] = c_new
                    if not has_state:
                        n_new = jnp.sum(kw, axis=0, keepdims=True)
                        if not no_carry:
                            n_new = n_new + nst[gi:gi + 1, :] * carry4[h:h + 1, :]
                        nst[gi:gi + 1, :] = n_new

                if d == 0:
                    acc_a[rows, hc] = h_dir
                else:
                    y = _head_norm(acc_a[rows, hc] + h_dir, mnw_ref[:, hc])
                    o = p[rows, OFF_MO + h * hd:OFF_MO + (h + 1) * hd].astype(F32)
                    hm_ref[rows, hc] = (y * jax.nn.sigmoid(o)).astype(BF16)

                lg = log_gamma[h]
                rq = rqs[rows, hc]
                rk = rks[rows, hc]
                rv = p[rows, OFF_RV + h * hd:OFF_RV + (h + 1) * hd]
                if d == 0:
                    scores = _dot_nt(rq, rk) * dsc[h]
                    y_dir = _dot(scores.astype(BF16), rv)
                    xi = jnp.exp(lg * (pos_col + 1.0))
                    zeta = jnp.exp(lg * ((ch - 1.0) - pos_col))
                else:
                    y_dir = None
                    xi = jnp.exp(lg * (ch - pos_col))
                    zeta = jnp.exp(lg * pos_col)
                if not no_carry:
                    s0 = sst[h]
                    y_int = _dot(rq, s0.astype(BF16)) * xi
                    y_dir = y_int if y_dir is None else y_dir + y_int
                if need_update:
                    s_new = _dot_tn((rk.astype(F32) * zeta).astype(BF16), rv)
                    if not no_carry:
                        s_new = s_new + s0 * jnp.exp(lg * float(ch))
                    sst[h] = s_new

                hcc = slice(2 * BRANCH_W + h * hd, 2 * BRANCH_W + (h + 1) * hd)
                if d == 0:
                    acc_c[rows, hc] = y_dir
                else:
                    y_sum = acc_c[rows, hc] if y_dir is None else acc_c[rows, hc] + y_dir
                    y = _head_norm(y_sum, rnw_ref[:, hc])
                    gt = p[rows, OFF_RG + h * hd:OFF_RG + (h + 1) * hd].astype(F32)
                    hm_ref[rows, hcc] = (y * (gt * jax.nn.sigmoid(gt))).astype(BF16)

            if need_update:
                mst[r0:r0 + N_HEADS, :] = jnp.broadcast_to(m_new4, (N_HEADS, LANES))

        if not has_state:
            for h in range(N_HEADS):
                cn_ref[d, h] = cst[h, :, 0:hd]
                sn_ref[d, h] = sst[h]
            nn_ref[d * N_HEADS:(d + 1) * N_HEADS, :] = nst[r0:r0 + N_HEADS, :]
            mn_ref[d * N_HEADS:(d + 1) * N_HEADS, :] = mst[r0:r0 + N_HEADS, :]

    while fillers:
        emit_filler()


def _mix_scratch(seq_len, ch):
    return [
        pltpu.VMEM((seq_len, BRANCH_W), F32),
        pltpu.VMEM((seq_len, BRANCH_W), F32),
        pltpu.VMEM((seq_len + 32, BRANCH_W), F32),
        pltpu.VMEM((SUBLANES, CONV_WIN, BRANCH_W), F32),
        pltpu.VMEM((seq_len, BRANCH_W), BF16),
        pltpu.VMEM((seq_len, BRANCH_W), BF16),
        pltpu.VMEM((N_HEADS, HEAD_DIM, 2 * HEAD_DIM), F32),
        pltpu.VMEM((N_GATE_COLS, HEAD_DIM), F32),
        pltpu.VMEM((N_GATE_COLS, LANES), F32),
        pltpu.VMEM((N_HEADS, HEAD_DIM, HEAD_DIM), F32),
        pltpu.VMEM((N_HEADS, ch, ch), F32),
    ]


N_MIX_PARAMS = 8
N_MIX_SCRATCH = 11


def _mix_param_specs(ch, index_map):
    def full(shape):
        return pl.BlockSpec(shape, lambda *a: (0,) * len(shape))

    del index_map
    return [full((N_GATE_COLS, 1)), full((1, BRANCH_W)), full((32, BRANCH_W)), full((1, BRANCH_W)),
            full((1, BRANCH_W)), full((1, BRANCH_W)), full((8, ch)), full((1, BRANCH_W))]


def _mix_param_args(params):
    return [params["gate_b_row"], params["mlstm_norm_w"], params["conv_w"], params["conv_b"],
            params["conv_ln_w"], params["conv_ln_b"], params["ret_decay"], params["ret_norm_w"]]


def _seqmix_latent_kernel(*refs, seq_len, ch, layer):
    p_ref, gr_ref = refs[:2]
    prm = refs[2:2 + N_MIX_PARAMS]
    pos = 2 + N_MIX_PARAMS
    state_in = refs[pos:pos + 4]
    hm_ref = refs[pos + 4]
    scr = refs[pos + 5:pos + 5 + N_MIX_SCRATCH]
    rope_tabs = refs[pos + 5 + N_MIX_SCRATCH:]
    b_idx = pl.program_id(0)

    @pl.when(b_idx == 0)
    def _():
        _init_tables(prm[6], scr[10], rope_tabs, seq_len=seq_len, ch=ch)

    _mix_sequence(p_ref, gr_ref, prm, state_in, hm_ref, None, scr, rope_tabs,
                  seq_len=seq_len, ch=ch, layer=layer, seq_idx=b_idx)


def _seq_mix_latent(p, gr, seq_len, layer, params, state):
    rows = p.shape[0]
    bsz = rows // seq_len
    ch = SCAN_CHUNK
    n_ch = seq_len // ch
    c0, n0, m0, s0 = state
    mat = pl.BlockSpec((None, None, 2, N_HEADS, HEAD_DIM, HEAD_DIM), lambda b: (b, layer, 0, 0, 0, 0))
    in_specs = (
        [pl.BlockSpec((seq_len, N_MIX), lambda b: (b, 0)),
         pl.BlockSpec((n_ch, N_GATE_COLS, ch), lambda b: (b, 0, 0))]
        + _mix_param_specs(ch, None)
        + [mat, pl.BlockSpec((None, None, 2 * N_HEADS, HEAD_DIM), lambda b: (b, layer, 0, 0)),
           pl.BlockSpec(memory_space=pltpu.SMEM), mat])
    args = [p, gr] + _mix_param_args(params) + [
        c0, n0.reshape(n0.shape[0], n0.shape[1], 2 * N_HEADS, HEAD_DIM), m0, s0]
    scratch = _mix_scratch(seq_len, ch) + [pltpu.VMEM((seq_len, HEAD_DIM), F32),
                                           pltpu.VMEM((seq_len, HEAD_DIM), F32)]
    return pl.pallas_call(
        functools.partial(_seqmix_latent_kernel, seq_len=seq_len, ch=ch, layer=layer),
        grid=(bsz,),
        in_specs=in_specs,
        out_specs=pl.BlockSpec((seq_len, 3 * BRANCH_W), lambda b: (b, 0)),
        out_shape=jax.ShapeDtypeStruct((rows, 3 * BRANCH_W), BF16),
        scratch_shapes=scratch,
        compiler_params=pltpu.CompilerParams(dimension_semantics=("arbitrary",)),
        name="seq_mix_latent",
    )(*args)


def _mixer_context_kernel(*refs, seq_len, ch, layer):
    x_ref, mod_ref, w_ref, wgt_ref = refs[:4]
    prm = refs[4:4 + N_MIX_PARAMS]
    pos = 4 + N_MIX_PARAMS
    hm_ref = refs[pos]
    state_out = refs[pos + 1:pos + 5]
    p_cur, p_next, gr_cur, gr_next, h_scr = refs[pos + 5:pos + 10]
    scr = refs[pos + 10:pos + 10 + N_MIX_SCRATCH]
    s_idx = pl.program_id(0)

    @pl.when(s_idx == 0)
    def _():
        _init_tables(prm[6], scr[10], None, seq_len=seq_len, ch=ch)
        _project(x_ref, mod_ref, w_ref, wgt_ref, p_next, gr_next, h_scr, ch=ch)

    @pl.when(s_idx > 0)
    def _():
        for jc in range(N_MIX // PROJ_TILE):
            cs = slice(jc * PROJ_TILE, (jc + 1) * PROJ_TILE)
            p_cur[:, cs] = p_next[:, cs]
        gr_cur[...] = gr_next[...]
        pieces = _project_pieces(x_ref, mod_ref, w_ref, wgt_ref, p_next, gr_next, h_scr, ch=ch)
        pieces[0]()
        _mix_sequence(p_cur, gr_cur, prm, None, hm_ref, state_out, scr, None,
                      seq_len=seq_len, ch=ch, layer=layer, seq_idx=s_idx - 1, fillers=pieces[1:])


def _mixer_context(x2d, mod, seq_len, layer, wp, wgt, params):
    rows = x2d.shape[0]
    bsz = rows // seq_len
    ch = SCAN_CHUNK
    n_ch = seq_len // ch

    def nxt(s):
        return jnp.minimum(s, bsz - 1)

    def cur(s):
        return jnp.maximum(s - 1, 0)

    mat = pl.BlockSpec((None, 2, N_HEADS, HEAD_DIM, HEAD_DIM), lambda s: (cur(s), 0, 0, 0, 0))
    vec = pl.BlockSpec((None, 2 * N_HEADS, HEAD_DIM), lambda s: (cur(s), 0, 0))
    in_specs = (
        [pl.BlockSpec((seq_len, D_MODEL), lambda s: (nxt(s), 0)),
         pl.BlockSpec((None, 6, D_MODEL), lambda s: (0, 0, 0)),
         _weight_spec((D_MODEL, N_MIX), layer),
         _weight_spec((N_GATE_COLS, D_MODEL), layer)]
        + _mix_param_specs(ch, None))
    out_specs = [pl.BlockSpec((seq_len, 3 * BRANCH_W), lambda s: (cur(s), 0)), mat, vec, vec, mat]
    out_shape = [
        jax.ShapeDtypeStruct((rows, 3 * BRANCH_W), BF16),
        jax.ShapeDtypeStruct((bsz, 2, N_HEADS, HEAD_DIM, HEAD_DIM), F32),
        jax.ShapeDtypeStruct((bsz, 2 * N_HEADS, HEAD_DIM), F32),
        jax.ShapeDtypeStruct((bsz, 2 * N_HEADS, LANES), F32),
        jax.ShapeDtypeStruct((bsz, 2, N_HEADS, HEAD_DIM, HEAD_DIM), F32),
    ]
    scratch = [
        pltpu.VMEM((seq_len, N_MIX), BF16),
        pltpu.VMEM((seq_len, N_MIX), BF16),
        pltpu.VMEM((n_ch, N_GATE_COLS, ch), F32),
        pltpu.VMEM((n_ch, N_GATE_COLS, ch), F32),
        pltpu.VMEM((seq_len, D_MODEL), BF16),
    ] + _mix_scratch(seq_len, ch)
    return pl.pallas_call(
        functools.partial(_mixer_context_kernel, seq_len=seq_len, ch=ch, layer=layer),
        grid=(bsz + 1,),
        in_specs=in_specs,
        out_specs=out_specs,
        out_shape=out_shape,
        scratch_shapes=scratch,
        compiler_params=pltpu.CompilerParams(dimension_semantics=("arbitrary",)),
        name="mixer_context",
    )(x2d, mod, wp, wgt, *_mix_param_args(params))


def _post_kernel(hm_ref, x_ref, mod_ref, wgm_ref, wa_ref, wb_ref, wc_ref, wo_ref,
                 l1w_ref, l1b_ref, w13_ref, w2_ref, l2w_ref, l2b_ref, o_ref, *, alpha):
    shift1 = mod_ref[0:1, :]
    scale1 = mod_ref[1:2, :]
    gate1 = mod_ref[2:3, :]
    shift2 = mod_ref[3:4, :]
    scale2 = mod_ref[4:5, :]
    gate2 = mod_ref[5:6, :]
    x = x_ref[...]
    h1 = (x * (1.0 + scale1) + shift1).astype(BF16)
    merged = None
    for g, w_ref in enumerate((wa_ref, wb_ref, wc_ref)):
        gate = jax.nn.sigmoid(_dot(h1, wgm_ref[:, g * D_MODEL:(g + 1) * D_MODEL]))
        term = gate * _dot(hm_ref[:, g * BRANCH_W:(g + 1) * BRANCH_W], w_ref[...])
        merged = term if merged is None else merged + term
    mix = _dot(merged.astype(BF16), wo_ref[...])
    x1 = _layer_norm(alpha * x + gate1 * mix, l1w_ref[...], l1b_ref[...])
    h2 = (x1 * (1.0 + scale2) + shift2).astype(BF16)
    ff = jnp.zeros(x1.shape, F32)
    for c in range(FFN_HIDDEN // FFN_TILE):
        a = _dot(h2, w13_ref[:, c * FFN_TILE:(c + 1) * FFN_TILE])
        gt = _dot(h2, w13_ref[:, FFN_HIDDEN + c * FFN_TILE:FFN_HIDDEN + (c + 1) * FFN_TILE])
        act = (gt * jax.nn.sigmoid(gt)) * a
        ff = ff + _dot(act.astype(BF16), w2_ref[c * FFN_TILE:(c + 1) * FFN_TILE, :])
    o_ref[...] = _layer_norm(alpha * x1 + gate2 * ff, l2w_ref[...], l2b_ref[...])


def _post(hm, x2d, mod, seq_len, layer, w, alpha):
    rows = x2d.shape[0]
    tm = ROW_TILE
    n_seq = mod.shape[0]

    def seq_of(i):
        return (i * tm) // seq_len if n_seq > 1 else 0

    vec = _weight_spec((1, D_MODEL), layer)
    return pl.pallas_call(
        functools.partial(_post_kernel, alpha=alpha),
        grid=(rows // tm,),
        in_specs=[
            pl.BlockSpec((tm, 3 * BRANCH_W), lambda i: (i, 0)),
            pl.BlockSpec((tm, D_MODEL), lambda i: (i, 0)),
            pl.BlockSpec((None, 6, D_MODEL), lambda i: (seq_of(i), 0, 0)),
            _weight_spec((D_MODEL, N_GM), layer),
            _weight_spec((BRANCH_W, D_MODEL), layer), _weight_spec((BRANCH_W, D_MODEL), layer),
            _weight_spec((BRANCH_W, D_MODEL), layer),
            _weight_spec((D_MODEL, D_MODEL), layer), vec, vec,
            _weight_spec((D_MODEL, 2 * FFN_HIDDEN), layer), _weight_spec((FFN_HIDDEN, D_MODEL), layer), vec, vec,
        ],
        out_specs=pl.BlockSpec((tm, D_MODEL), lambda i: (i, 0)),
        out_shape=jax.ShapeDtypeStruct((rows, D_MODEL), F32),
        compiler_params=pltpu.CompilerParams(dimension_semantics=("parallel",)),
        name="post",
    )(hm, x2d, mod, w["gm_w"], w["mlstm_out_w"], w["conv_out_w"], w["ret_out_w"], w["out_w"],
      w["ln1_w"], w["ln1_b"], w["ffn_w13"], w["ffn_w2"], w["ln2_w"], w["ln2_b"])


def kernel(x_prompt, x_sample, state_mlstm_C, state_mlstm_n, state_mlstm_m, state_ret_S, c, c_ctx, ada_w, ada_b, in_w, mlstm_gate_b, mlstm_norm_w, mlstm_out_w, conv_w, conv_b, conv_ln_w, conv_ln_b, conv_out_w, ret_decay, ret_norm_w, ret_out_w, out_w, ln1_w, ln1_b, ln2_w, ln2_b, ffn_w13, ffn_w2):
    depth = in_w.shape[0]
    alpha = (2.0 * depth) ** 0.25
    bsz, seq, _ = x_prompt.shape
    dbsz, dseq, _ = x_sample.shape
    ch = SCAN_CHUNK
    gate_hi = GATE_LO + N_GATE_COLS

    wp = jnp.concatenate([in_w[:, :, :GATE_LO], in_w[:, :, gate_hi:gate_hi + N_MIX - GATE_LO]], axis=2).astype(BF16)
    wgt = jnp.swapaxes(in_w[:, :, GATE_LO:gate_hi].astype(BF16), 1, 2)
    dense = {
        "gm_w": in_w[:, :, gate_hi + N_MIX - GATE_LO:].astype(BF16),
        "mlstm_out_w": mlstm_out_w.astype(BF16), "conv_out_w": conv_out_w.astype(BF16),
        "ret_out_w": ret_out_w.astype(BF16), "out_w": out_w.astype(BF16),
        "ffn_w13": ffn_w13.astype(BF16), "ffn_w2": ffn_w2.astype(BF16),
        "ln1_w": ln1_w.reshape(depth, 1, D_MODEL), "ln1_b": ln1_b.reshape(depth, 1, D_MODEL),
        "ln2_w": ln2_w.reshape(depth, 1, D_MODEL), "ln2_b": ln2_b.reshape(depth, 1, D_MODEL),
    }
    gate_b_flat = mlstm_gate_b.reshape(depth, N_GATE_COLS)

    c_all = jnp.concatenate([c_ctx[None, :], c, jnp.zeros((8 - 1 - dbsz, D_MODEL), F32)], axis=0)
    mod = _ada_mod(c_all, ada_w, ada_b)

    y_prompt = x_prompt.reshape(bsz * seq, D_MODEL)
    y_sample = x_sample.reshape(dbsz * dseq, D_MODEL)
    new_c, new_n, new_m, new_s = [], [], [], []
    for l in range(depth):
        mix_params = {
            "gate_b_row": gate_b_flat[l][:, None],
            "mlstm_norm_w": mlstm_norm_w[l][None, :],
            "conv_w": jnp.pad(conv_w[l], ((0, 32 - CONV_K), (0, 0))),
            "conv_b": conv_b[l][None, :],
            "conv_ln_w": conv_ln_w[l][None, :],
            "conv_ln_b": conv_ln_b[l][None, :],
            "ret_decay": jnp.broadcast_to(ret_decay[l].reshape(2 * N_HEADS, 1), (2 * N_HEADS, ch)),
            "ret_norm_w": ret_norm_w[l][None, :],
        }
        mod_ctx = mod[l, 0:1].reshape(1, 6, D_MODEL)
        mod_lat = mod[l, 1:1 + dbsz].reshape(dbsz, 6, D_MODEL)

        hm, cn, nn, mn, sn = _mixer_context(y_prompt, mod_ctx, seq, l, wp, wgt, mix_params)
        y_prompt = _post(hm, y_prompt, mod_ctx, seq, l, dense, alpha)
        new_c.append(cn)
        new_n.append(nn.reshape(bsz, 2, N_HEADS, HEAD_DIM))
        new_m.append(mn[:, :, 0].reshape(bsz, 2, N_HEADS))
        new_s.append(sn)

        p, gr = _in_proj(y_sample, mod_lat, dseq, l, wp, wgt)
        hm = _seq_mix_latent(p, gr, dseq, l, mix_params,
                             (state_mlstm_C, state_mlstm_n, state_mlstm_m, state_ret_S))
        y_sample = _post(hm, y_sample, mod_lat, dseq, l, dense, alpha)

    return (y_prompt.reshape(bsz, seq, D_MODEL), y_sample.reshape(dbsz, dseq, D_MODEL),
            jnp.stack(new_c, 1), jnp.stack(new_n, 1), jnp.stack(new_m, 1), jnp.stack(new_s, 1))
```

```python
import functools
import math

import jax
import jax.numpy as jnp
from jax import lax
from jax.experimental import pallas as pl
from jax.experimental.pallas import tpu as pltpu

F32 = jnp.float32
BF16 = jnp.bfloat16

D_MODEL = 1024
N_HEADS = 4
HEAD_DIM = 128
BRANCH_W = N_HEADS * HEAD_DIM
N_GATE_COLS = 4 * N_HEADS
GATE_LO = 4 * BRANCH_W
CONV_K = 31
CONV_PAD = CONV_K // 2
FFN_HIDDEN = 2816
GRID_W = 64
ROPE_BASE = 10000.0
LN_EPS = 1e-5
N_MIX = 10 * BRANCH_W
N_GM = 3 * D_MODEL
LANES = 128
SUBLANES = 8

OFF_MQ, OFF_MK, OFF_MV, OFF_MO = 0, 512, 1024, 1536
OFF_CA, OFF_CG = 2048, 2560
OFF_RQ, OFF_RK, OFF_RV, OFF_RG = 3072, 3584, 4096, 4608

ROW_TILE = 512
PROJ_TILE = 1024
SCAN_CHUNK = 256
CONV_ROWS = 128
CONV_WIN = CONV_ROWS + 24
FFN_TILE = 256


def _dot(a, b):
    return jnp.dot(a, b, preferred_element_type=F32)


def _dot_nt(a, b):
    return lax.dot_general(a, b, (((1,), (1,)), ((), ())), preferred_element_type=F32)


def _dot_tn(a, b):
    return lax.dot_general(a, b, (((0,), (0,)), ((), ())), preferred_element_type=F32)


def _layer_norm(z, w, b):
    mu = jnp.mean(z, axis=-1, keepdims=True)
    zc = z - mu
    var = jnp.mean(zc * zc, axis=-1, keepdims=True)
    return zc * lax.rsqrt(var + LN_EPS) * w + b


def _head_norm(y, w):
    mu = jnp.mean(y, axis=-1, keepdims=True)
    yc = y - mu
    var = jnp.mean(yc * yc, axis=-1, keepdims=True)
    return yc * lax.rsqrt(var + LN_EPS) * w


def _log_sigmoid(x):
    return jnp.minimum(x, 0.0) - jnp.log1p(jnp.exp(-jnp.abs(x)))


def _split3(x):
    hi = x.astype(BF16)
    r = x - hi.astype(F32)
    mid = r.astype(BF16)
    lo = (r - mid.astype(F32)).astype(BF16)
    return hi, mid, lo


def _exact_right(x, sel):
    hi, mid, lo = _split3(x)
    return _dot(hi, sel) + _dot(mid, sel) + _dot(lo, sel)


def _exact_transpose(x, sel):
    hi, mid, lo = _split3(x)
    return _dot_tn(hi, sel) + _dot_tn(mid, sel) + _dot_tn(lo, sel)


def _ada_kernel(c_ref, w_ref, b_ref, o_ref):
    c = c_ref[...]
    s = c * jax.nn.sigmoid(c)
    o_ref[...] = _dot(s.astype(BF16), w_ref[...].astype(BF16)) + b_ref[...]


def _ada_mod(c_all, ada_w, ada_b):
    depth, _, n = ada_w.shape
    tn = 1024
    return pl.pallas_call(
        _ada_kernel,
        grid=(depth, n // tn),
        in_specs=[
            pl.BlockSpec((8, D_MODEL), lambda l, j: (0, 0)),
            pl.BlockSpec((None, D_MODEL, tn), lambda l, j: (l, 0, j)),
            pl.BlockSpec((None, 1, tn), lambda l, j: (l, 0, j)),
        ],
        out_specs=pl.BlockSpec((None, 8, tn), lambda l, j: (l, 0, j)),
        out_shape=jax.ShapeDtypeStruct((depth, 8, n), F32),
        name="ada_mod",
    )(c_all, ada_w, ada_b.reshape(depth, 1, n))


def _project_pieces(x_ref, mod_ref, w_ref, wgt_ref, p_out, gr_out, h_scr, *, ch):
    n_rows = x_ref.shape[0]

    def modulate():
        shift = mod_ref[0:1, :]
        scale = mod_ref[1:2, :]
        h_scr[...] = (x_ref[...] * (1.0 + scale) + shift).astype(BF16)
        for cc in range(n_rows // ch):
            gr_out[cc] = _dot_nt(wgt_ref[...], h_scr[cc * ch:(cc + 1) * ch, :])

    def column_tile(jc):
        cs = slice(jc * PROJ_TILE, (jc + 1) * PROJ_TILE)
        p_out[:, cs] = _dot(h_scr[...], w_ref[:, cs]).astype(BF16)

    return [modulate] + [functools.partial(column_tile, jc) for jc in range(N_MIX // PROJ_TILE)]


def _project(x_ref, mod_ref, w_ref, wgt_ref, p_out, gr_out, h_scr, *, ch):
    for piece in _project_pieces(x_ref, mod_ref, w_ref, wgt_ref, p_out, gr_out, h_scr, ch=ch):
        piece()


def _inproj_kernel(x_ref, mod_ref, w_ref, wgt_ref, p_ref, gr_ref, h_scr, *, ch):
    _project(x_ref, mod_ref, w_ref, wgt_ref, p_ref, gr_ref, h_scr, ch=ch)


def _weight_spec(shape, layer):
    nd = len(shape)
    return pl.BlockSpec((None,) + shape, lambda *_: (layer,) + (0,) * nd, pipeline_mode=pl.Buffered(1))


def _in_proj(x2d, mod, seq_len, layer, wp, wgt):
    rows = x2d.shape[0]
    tm, ch = ROW_TILE, SCAN_CHUNK
    n_seq = mod.shape[0]

    def seq_of(i):
        return (i * tm) // seq_len if n_seq > 1 else 0

    return pl.pallas_call(
        functools.partial(_inproj_kernel, ch=ch),
        grid=(rows // tm,),
        in_specs=[
            pl.BlockSpec((tm, D_MODEL), lambda i: (i, 0)),
            pl.BlockSpec((None, 6, D_MODEL), lambda i: (seq_of(i), 0, 0)),
            _weight_spec((D_MODEL, N_MIX), layer),
            _weight_spec((N_GATE_COLS, D_MODEL), layer),
        ],
        out_specs=[
            pl.BlockSpec((tm, N_MIX), lambda i: (i, 0)),
            pl.BlockSpec((tm // ch, N_GATE_COLS, ch), lambda i: (i, 0, 0)),
        ],
        out_shape=[
            jax.ShapeDtypeStruct((rows, N_MIX), BF16),
            jax.ShapeDtypeStruct((rows // ch, N_GATE_COLS, ch), F32),
        ],
        scratch_shapes=[pltpu.VMEM((tm, D_MODEL), BF16)],
        compiler_params=pltpu.CompilerParams(dimension_semantics=("parallel",)),
        name="in_proj",
    )(x2d, mod, wp, wgt)


def _init_tables(dec_ref, dsc, rope_tabs, *, seq_len, ch, scale=None):
    hd = HEAD_DIM
    row_i = lax.broadcasted_iota(jnp.int32, (ch, ch), 0)
    col_i = lax.broadcasted_iota(jnp.int32, (ch, ch), 1)
    diff = (row_i - col_i).astype(F32)
    for h in range(N_HEADS):
        lg_f = _log_sigmoid(dec_ref[h:h + 1, :])
        lg_b = _log_sigmoid(dec_ref[N_HEADS + h:N_HEADS + h + 1, :])
        decay = (jnp.where(col_i <= row_i, jnp.exp(lg_f * jnp.maximum(diff, 0.0)), 0.0)
                 + jnp.where(col_i >= row_i, jnp.exp(lg_b * jnp.maximum(-diff, 0.0)), 0.0))
        dsc[h] = decay if scale is None else decay * scale
    if rope_tabs is not None:
        cos_t, sin_t = rope_tabs
        t_i = lax.broadcasted_iota(jnp.int32, (seq_len, hd), 0)
        lane = lax.broadcasted_iota(jnp.int32, (seq_len, hd), 1)
        pair = lane >> 1
        n_pairs = hd // 4
        freq = jnp.exp((pair & (n_pairs - 1)).astype(F32) * (-math.log(ROPE_BASE) / n_pairs))
        grid_pos = jnp.where(pair < n_pairs, t_i // GRID_W, t_i % GRID_W).astype(F32)
        ang = grid_pos * freq
        sin = jnp.sin(ang)
        cos_t[...] = jnp.cos(ang)
        sin_t[...] = jnp.where((lane & 1) == 0, -sin, sin)


def _conv_module(p, cw_ref, cb_ref, clw_ref, clb_ref, hm_ref, upad, shf, seq_len, emit_filler):
    zeros_pad = jnp.zeros((16, BRANCH_W), F32)
    upad[0:16, :] = zeros_pad
    upad[seq_len + 16:seq_len + 32, :] = zeros_pad
    for rb in range(seq_len // CONV_ROWS):
        r = slice(rb * CONV_ROWS, (rb + 1) * CONV_ROWS)
        ca = p[r, OFF_CA:OFF_CA + BRANCH_W].astype(F32)
        cg = p[r, OFF_CG:OFF_CG + BRANCH_W].astype(F32)
        upad[16 + rb * CONV_ROWS:16 + (rb + 1) * CONV_ROWS, :] = ca * jax.nn.sigmoid(cg)
    for rb in range(seq_len // CONV_ROWS):
        emit_filler()
        base = rb * CONV_ROWS
        for sft in range(1, SUBLANES):
            shf[sft] = upad[base + sft:base + sft + CONV_WIN, :]
        acc = None
        for k in range(CONV_K):
            a8, sft = divmod(k + 16 - CONV_PAD, SUBLANES)
            if sft == 0:
                win = upad[base + a8 * SUBLANES:base + a8 * SUBLANES + CONV_ROWS, :]
            else:
                win = shf[sft, a8 * SUBLANES:a8 * SUBLANES + CONV_ROWS, :]
            term = win * cw_ref[k:k + 1, :]
            acc = term if acc is None else acc + term
        acc = acc + cb_ref[...]
        u = _layer_norm(acc, clw_ref[...], clb_ref[...])
        u = u * jax.nn.sigmoid(u)
        hm_ref[rb * CONV_ROWS:(rb + 1) * CONV_ROWS, BRANCH_W:2 * BRANCH_W] = u.astype(BF16)


def _mix_context_sequence(p, gr, prm, hm_ref, state_out, upad, shf, dsc, *, seq_len, fillers=()):
    fillers = list(fillers)

    def emit_filler():
        if fillers:
            fillers.pop(0)()

    gbr_ref, mnw_ref, cw_ref, cb_ref, clw_ref, clb_ref, dec_ref, rnw_ref = prm
    cn_ref, nn_ref, mn_ref, sn_ref = state_out
    ch = seq_len
    hd = HEAD_DIM
    k_scale = HEAD_DIM ** -0.5
    log_k = math.log(k_scale)
    half = 2 * N_HEADS

    row_i = lax.broadcasted_iota(jnp.int32, (ch, ch), 0)
    col_i = lax.broadcasted_iota(jnp.int32, (ch, ch), 1)
    lower = col_i <= row_i
    upper = col_i >= row_i
    tri_l = lower.astype(BF16)
    tri_u = upper.astype(BF16)
    pos_col = lax.broadcasted_iota(jnp.int32, (ch, 1), 0).astype(F32)
    gate_lane = lax.broadcasted_iota(jnp.int32, (N_GATE_COLS, ch), 1)
    fwd_rows = lax.broadcasted_iota(jnp.int32, (N_GATE_COLS, ch), 0) < half
    ones_col = (lax.broadcasted_iota(jnp.int32, (ch, hd), 1) == 0).astype(BF16)
    n_stack = 3 * N_GATE_COLS
    eye_t = (lax.broadcasted_iota(jnp.int32, (n_stack, LANES), 0)
             == lax.broadcasted_iota(jnp.int32, (n_stack, LANES), 1)).astype(BF16)

    g = gr[0] + gbr_ref[...]
    lf_hi, lf_mid, lf_lo = _split3(_log_sigmoid(g))
    prefix = _dot(lf_hi, tri_u) + _dot(lf_mid, tri_u) + _dot(lf_lo, tri_u)
    suffix = _dot(lf_hi, tri_l) + _dot(lf_mid, tri_l) + _dot(lf_lo, tri_l)
    cum = jnp.where(fwd_rows, prefix, suffix)
    b_rows = pltpu.roll(cum, N_GATE_COLS - N_HEADS, 0)
    a_all = g - b_rows
    cm_f = a_all
    cm_b = a_all
    sft = 1
    while sft < ch:
        cm_f = jnp.maximum(cm_f, jnp.where(gate_lane >= sft, pltpu.roll(cm_f, sft, 1), -jnp.inf))
        cm_b = jnp.maximum(cm_b, jnp.where(gate_lane < ch - sft, pltpu.roll(cm_b, ch - sft, 1), -jnp.inf))
        sft *= 2
    big_m = jnp.maximum(jnp.where(fwd_rows, cm_f, cm_b), 0.0)
    m_true = b_rows + big_m
    cols = _exact_transpose(jnp.concatenate([a_all, big_m, m_true], axis=0), eye_t)
    edges = (ch - 1, 0)
    m_last = [big_m[half * d:half * d + N_HEADS, edges[d]:edges[d] + 1] for d in range(2)]
    for d in range(2):
        m_new = cum[half * d + N_HEADS:half * (d + 1), edges[d]:edges[d] + 1] + m_last[d]
        mn_ref[d * N_HEADS:(d + 1) * N_HEADS, :] = jnp.broadcast_to(m_new, (N_HEADS, LANES))

    _conv_module(p, cw_ref, cb_ref, clw_ref, clb_ref, hm_ref, upad, shf, seq_len, emit_filler)

    for h in range(N_HEADS):
        emit_filler()
        hc = slice(h * hd, (h + 1) * hd)

        q = p[:, OFF_MQ + h * hd:OFF_MQ + (h + 1) * hd]
        k = p[:, OFF_MK + h * hd:OFF_MK + (h + 1) * hd]
        v = p[:, OFF_MV + h * hd:OFF_MV + (h + 1) * hd]
        v_aug = jnp.concatenate([v, ones_col], axis=1)
        qk = _dot_nt(q, k)
        kf = k.astype(F32)
        h_sum = None
        for d in range(2):
            gi = half * d + h
            mask = lower if d == 0 else upper
            a_r = a_all[gi:gi + 1, :] + log_k
            a_c = cols[:, gi:gi + 1]
            bm_c = cols[:, N_GATE_COLS + gi:N_GATE_COLS + gi + 1]
            mt_c = cols[:, 2 * N_GATE_COLS + gi:2 * N_GATE_COLS + gi + 1]
            s = qk * jnp.exp(jnp.where(mask, a_r - bm_c, -jnp.inf))
            nd = _dot(s.astype(BF16), v_aug)
            h_dir = nd[:, 0:hd] / jnp.maximum(jnp.abs(nd[:, hd:hd + 1]), jnp.exp(-mt_c))
            h_sum = h_dir if h_sum is None else h_sum + h_dir
            kw = kf * (jnp.exp(a_c - m_last[d][h:h + 1, :]) * k_scale)
            cn_ref[d, h] = _dot_tn(kw.astype(BF16), v)
            nn_ref[d * N_HEADS + h:d * N_HEADS + h + 1, :] = jnp.sum(kw, axis=0, keepdims=True)
        y = _head_norm(h_sum, mnw_ref[:, hc])
        o = p[:, OFF_MO + h * hd:OFF_MO + (h + 1) * hd].astype(F32)
        hm_ref[:, hc] = (y * jax.nn.sigmoid(o)).astype(BF16)

        rq = p[:, OFF_RQ + h * hd:OFF_RQ + (h + 1) * hd]
        rk = p[:, OFF_RK + h * hd:OFF_RK + (h + 1) * hd]
        rv = p[:, OFF_RV + h * hd:OFF_RV + (h + 1) * hd]
        scores = _dot_nt(rq, rk) * dsc[h]
        y = _head_norm(_dot(scores.astype(BF16), rv), rnw_ref[:, hc])
        gt = p[:, OFF_RG + h * hd:OFF_RG + (h + 1) * hd].astype(F32)
        hm_ref[:, 2 * BRANCH_W + h * hd:2 * BRANCH_W + (h + 1) * hd] = (y * (gt * jax.nn.sigmoid(gt))).astype(BF16)
        rkf = rk.astype(F32)
        for d in range(2):
            lg = _log_sigmoid(dec_ref[d * N_HEADS + h:d * N_HEADS + h + 1, 0:1])
            zeta = jnp.exp(lg * ((ch - 1.0) - pos_col)) if d == 0 else jnp.exp(lg * pos_col)
            sn_ref[d, h] = _dot_tn((rkf * (zeta * k_scale)).astype(BF16), rv)

    while fillers:
        emit_filler()


def _mix_sequence(p, gr, prm, state_in, hm_ref, state_out, scr, rope_tabs, *, seq_len, ch, layer, seq_idx,
                  fillers=()):
    fillers = list(fillers)

    def emit_filler():
        if fillers:
            fillers.pop(0)()

    gbr_ref, mnw_ref, cw_ref, cb_ref, clw_ref, clb_ref, dec_ref, rnw_ref = prm
    acc_a, acc_c, upad, shf, rqs, rks, cst, nst, mst, sst, dsc = scr
    has_state = state_in is not None
    if has_state:
        c0_ref, n0_ref, m0_ref, s0_ref = state_in
    else:
        cn_ref, nn_ref, mn_ref, sn_ref = state_out

    n_ch = seq_len // ch
    hd = HEAD_DIM
    k_scale = HEAD_DIM ** -0.5

    row_i = lax.broadcasted_iota(jnp.int32, (ch, ch), 0)
    col_i = lax.broadcasted_iota(jnp.int32, (ch, ch), 1)
    lower = col_i <= row_i
    upper = col_i >= row_i
    tri_l = lower.astype(BF16)
    tri_u = upper.astype(BF16)
    pos_col = lax.broadcasted_iota(jnp.int32, (ch, 1), 0).astype(F32)
    gate_lane = lax.broadcasted_iota(jnp.int32, (N_GATE_COLS, ch), 1)
    ones_col = (lax.broadcasted_iota(jnp.int32, (ch, hd), 1) == 0).astype(BF16)
    n_stack = 3 * N_GATE_COLS
    eye_t = (lax.broadcasted_iota(jnp.int32, (n_stack, LANES), 0)
             == lax.broadcasted_iota(jnp.int32, (n_stack, LANES), 1)).astype(BF16)

    _conv_module(p, cw_ref, cb_ref, clw_ref, clb_ref, hm_ref, upad, shf, seq_len, emit_filler)

    if rope_tabs is not None:
        cos_t, sin_t = rope_tabs
        even = (lax.broadcasted_iota(jnp.int32, (seq_len, hd), 1) & 1) == 0

        def rope(x):
            swapped = jnp.where(even, pltpu.roll(x, hd - 1, 1), pltpu.roll(x, 1, 1))
            return x * cos_t[...] + swapped * sin_t[...]
    else:
        def rope(x):
            return x

    for h in range(N_HEADS):
        hc = slice(h * hd, (h + 1) * hd)
        rqs[:, hc] = rope(p[:, OFF_RQ + h * hd:OFF_RQ + (h + 1) * hd].astype(F32)).astype(BF16)
        rk = rope(p[:, OFF_RK + h * hd:OFF_RK + (h + 1) * hd].astype(F32))
        rks[:, hc] = (rk * k_scale).astype(BF16)

    mst[...] = jnp.zeros(mst.shape, F32)

    for d in range(2):
        mask = lower if d == 0 else upper
        tri_row = tri_u if d == 0 else tri_l
        r0 = 2 * N_HEADS * d
        edge = ch - 1 if d == 0 else 0
        order = range(n_ch) if d == 0 else range(n_ch - 1, -1, -1)
        log_gamma = [_log_sigmoid(dec_ref[d * N_HEADS + h:d * N_HEADS + h + 1, 0:1]) for h in range(N_HEADS)]

        if has_state:
            n_hi, n_mid, n_lo = _split3(n0_ref[...])
            sel_row = lax.broadcasted_iota(jnp.int32, (2 * N_HEADS, LANES), 0)
            sel_lane = lax.broadcasted_iota(jnp.int32, (2 * N_HEADS, LANES), 1)
            for h in range(N_HEADS):
                sel = ((sel_row == d * N_HEADS + h) & (sel_lane == 0)).astype(BF16)
                cst[h, :, 0:hd] = c0_ref[d, h]
                cst[h, :, hd:2 * hd] = _dot_tn(n_hi, sel) + _dot_tn(n_mid, sel) + _dot_tn(n_lo, sel)
                sst[h] = s0_ref[d, h]
                mst[r0 + h:r0 + h + 1, :] = jnp.full((1, LANES), m0_ref[seq_idx, layer, d, h], F32)

        for step, c in enumerate(order):
            no_carry = (not has_state) and step == 0
            need_update = (step < n_ch - 1) or (not has_state)
            rows = slice(c * ch, (c + 1) * ch)

            g = gr[c] + gbr_ref[...]
            cum = _exact_right(_log_sigmoid(g), tri_row)
            b_rows = pltpu.roll(cum, N_GATE_COLS - N_HEADS, 0)
            a_all = g - b_rows
            cm = a_all
            sft = 1
            while sft < ch:
                if d == 0:
                    moved = jnp.where(gate_lane >= sft, pltpu.roll(cm, sft, 1), -jnp.inf)
                else:
                    moved = jnp.where(gate_lane < ch - sft, pltpu.roll(cm, ch - sft, 1), -jnp.inf)
                cm = jnp.maximum(cm, moved)
                sft *= 2
            m0_all = mst[:, 0:1]
            big_m = jnp.maximum(cm, m0_all)
            m_true = b_rows + big_m
            cols = _exact_transpose(jnp.concatenate([a_all, big_m, m_true], axis=0), eye_t)
            m_last4 = big_m[r0:r0 + N_HEADS, edge:edge + 1]
            m_new4 = cum[r0 + N_HEADS:r0 + 2 * N_HEADS, edge:edge + 1] + m_last4
            carry4 = jnp.exp(m0_all[r0:r0 + N_HEADS, :] - m_last4)

            for h in range(N_HEADS):
                emit_filler()
                hc = slice(h * hd, (h + 1) * hd)
                gi = r0 + h

                a_r = a_all[gi:gi + 1, :]
                a_c = cols[:, gi:gi + 1]
                bm_c = cols[:, N_GATE_COLS + gi:N_GATE_COLS + gi + 1]
                mt_c = cols[:, 2 * N_GATE_COLS + gi:2 * N_GATE_COLS + gi + 1]
                q = p[rows, OFF_MQ + h * hd:OFF_MQ + (h + 1) * hd]
                kf = p[rows, OFF_MK + h * hd:OFF_MK + (h + 1) * hd].astype(F32) * k_scale
                v = p[rows, OFF_MV + h * hd:OFF_MV + (h + 1) * hd]
                v_aug = jnp.concatenate([v, ones_col], axis=1)

                s = _dot_nt(q, kf.astype(BF16)) * jnp.exp(jnp.where(mask, a_r - bm_c, -jnp.inf))
                nd = _dot(s.astype(BF16), v_aug)
                if not no_carry:
                    c_aug = cst[h]
                    w_inter = jnp.exp(m0_all[gi:gi + 1, :] - bm_c)
                    nd = nd + _dot(q, c_aug.astype(BF16)) * w_inter
                h_dir = nd[:, 0:hd] / jnp.maximum(jnp.abs(nd[:, hd:hd + 1]), jnp.exp(-mt_c))

                if need_update:
                    kw = kf * jnp.exp(a_c - m_last4[h:h + 1, :])
                    c_new = _dot_tn(kw.astype(BF16), v_aug)
                    if not no_carry:
                        c_new = c_new + c_aug * carry4[h:h + 1, :]
                    cst[h] = c_new
                    if not has_state:
                        n_new = jnp.sum(kw, axis=0, keepdims=True)
                        if not no_carry:
                            n_new = n_new + nst[gi:gi + 1, :] * carry4[h:h + 1, :]
                        nst[gi:gi + 1, :] = n_new

                if d == 0:
                    acc_a[rows, hc] = h_dir
                else:
                    y = _head_norm(acc_a[rows, hc] + h_dir, mnw_ref[:, hc])
                    o = p[rows, OFF_MO + h * hd:OFF_MO + (h + 1) * hd].astype(F32)
                    hm_ref[rows, hc] = (y * jax.nn.sigmoid(o)).astype(BF16)

                lg = log_gamma[h]
                rq = rqs[rows, hc]
                rk = rks[rows, hc]
                rv = p[rows, OFF_RV + h * hd:OFF_RV + (h + 1) * hd]
                if d == 0:
                    scores = _dot_nt(rq, rk) * dsc[h]
                    y_dir = _dot(scores.astype(BF16), rv)
                    xi = jnp.exp(lg * (pos_col + 1.0))
                    zeta = jnp.exp(lg * ((ch - 1.0) - pos_col))
                else:
                    y_dir = None
                    xi = jnp.exp(lg * (ch - pos_col))
                    zeta = jnp.exp(lg * pos_col)
                if not no_carry:
                    s0 = sst[h]
                    y_int = _dot(rq, s0.astype(BF16)) * xi
                    y_dir = y_int if y_dir is None else y_dir + y_int
                if need_update:
                    s_new = _dot_tn((rk.astype(F32) * zeta).astype(BF16), rv)
                    if not no_carry:
                        s_new = s_new + s0 * jnp.exp(lg * float(ch))
                    sst[h] = s_new

                hcc = slice(2 * BRANCH_W + h * hd, 2 * BRANCH_W + (h + 1) * hd)
                if d == 0:
                    acc_c[rows, hc] = y_dir
                else:
                    y_sum = acc_c[rows, hc] if y_dir is None else acc_c[rows, hc] + y_dir
                    y = _head_norm(y_sum, rnw_ref[:, hc])
                    gt = p[rows, OFF_RG + h * hd:OFF_RG + (h + 1) * hd].astype(F32)
                    hm_ref[rows, hcc] = (y * (gt * jax.nn.sigmoid(gt))).astype(BF16)

            if need_update:
                mst[r0:r0 + N_HEADS, :] = jnp.broadcast_to(m_new4, (N_HEADS, LANES))

        if not has_state:
            for h in range(N_HEADS):
                cn_ref[d, h] = cst[h, :, 0:hd]
                sn_ref[d, h] = sst[h]
            nn_ref[d * N_HEADS:(d + 1) * N_HEADS, :] = nst[r0:r0 + N_HEADS, :]
            mn_ref[d * N_HEADS:(d + 1) * N_HEADS, :] = mst[r0:r0 + N_HEADS, :]

    while fillers:
        emit_filler()


def _mix_scratch(seq_len, ch):
    return [
        pltpu.VMEM((seq_len, BRANCH_W), F32),
        pltpu.VMEM((seq_len, BRANCH_W), F32),
        pltpu.VMEM((seq_len + 32, BRANCH_W), F32),
        pltpu.VMEM((SUBLANES, CONV_WIN, BRANCH_W), F32),
        pltpu.VMEM((seq_len, BRANCH_W), BF16),
        pltpu.VMEM((seq_len, BRANCH_W), BF16),
        pltpu.VMEM((N_HEADS, HEAD_DIM, 2 * HEAD_DIM), F32),
        pltpu.VMEM((N_GATE_COLS, HEAD_DIM), F32),
        pltpu.VMEM((N_GATE_COLS, LANES), F32),
        pltpu.VMEM((N_HEADS, HEAD_DIM, HEAD_DIM), F32),
        pltpu.VMEM((N_HEADS, ch, ch), F32),
    ]


N_MIX_PARAMS = 8
N_MIX_SCRATCH = 11


def _mix_param_specs(ch, index_map):
    def full(shape):
        return pl.BlockSpec(shape, lambda *a: (0,) * len(shape))

    del index_map
    return [full((N_GATE_COLS, 1)), full((1, BRANCH_W)), full((32, BRANCH_W)), full((1, BRANCH_W)),
            full((1, BRANCH_W)), full((1, BRANCH_W)), full((8, ch)), full((1, BRANCH_W))]


def _mix_param_args(params):
    return [params["gate_b_row"], params["mlstm_norm_w"], params["conv_w"], params["conv_b"],
            params["conv_ln_w"], params["conv_ln_b"], params["ret_decay"], params["ret_norm_w"]]


def _seqmix_latent_kernel(*refs, seq_len, ch, layer):
    p_ref, gr_ref = refs[:2]
    prm = refs[2:2 + N_MIX_PARAMS]
    pos = 2 + N_MIX_PARAMS
    state_in = refs[pos:pos + 4]
    hm_ref = refs[pos + 4]
    scr = refs[pos + 5:pos + 5 + N_MIX_SCRATCH]
    rope_tabs = refs[pos + 5 + N_MIX_SCRATCH:]
    b_idx = pl.program_id(0)

    @pl.when(b_idx == 0)
    def _():
        _init_tables(prm[6], scr[10], rope_tabs, seq_len=seq_len, ch=ch)

    _mix_sequence(p_ref, gr_ref, prm, state_in, hm_ref, None, scr, rope_tabs,
                  seq_len=seq_len, ch=ch, layer=layer, seq_idx=b_idx)


def _seq_mix_latent(p, gr, seq_len, layer, params, state):
    rows = p.shape[0]
    bsz = rows // seq_len
    ch = SCAN_CHUNK
    n_ch = seq_len // ch
    c0, n0, m0, s0 = state
    mat = pl.BlockSpec((None, None, 2, N_HEADS, HEAD_DIM, HEAD_DIM), lambda b: (b, layer, 0, 0, 0, 0))
    in_specs = (
        [pl.BlockSpec((seq_len, N_MIX), lambda b: (b, 0)),
         pl.BlockSpec((n_ch, N_GATE_COLS, ch), lambda b: (b, 0, 0))]
        + _mix_param_specs(ch, None)
        + [mat, pl.BlockSpec((None, None, 2 * N_HEADS, HEAD_DIM), lambda b: (b, layer, 0, 0)),
           pl.BlockSpec(memory_space=pltpu.SMEM), mat])
    args = [p, gr] + _mix_param_args(params) + [
        c0, n0.reshape(n0.shape[0], n0.shape[1], 2 * N_HEADS, HEAD_DIM), m0, s0]
    scratch = _mix_scratch(seq_len, ch) + [pltpu.VMEM((seq_len, HEAD_DIM), F32),
                                           pltpu.VMEM((seq_len, HEAD_DIM), F32)]
    return pl.pallas_call(
        functools.partial(_seqmix_latent_kernel, seq_len=seq_len, ch=ch, layer=layer),
        grid=(bsz,),
        in_specs=in_specs,
        out_specs=pl.BlockSpec((seq_len, 3 * BRANCH_W), lambda b: (b, 0)),
        out_shape=jax.ShapeDtypeStruct((rows, 3 * BRANCH_W), BF16),
        scratch_shapes=scratch,
        compiler_params=pltpu.CompilerParams(dimension_semantics=("arbitrary",)),
        name="seq_mix_latent",
    )(*args)


def _mixer_context_kernel(*refs, seq_len, ch, layer):
    x_ref, mod_ref, w_ref, wgt_ref = refs[:4]
    prm = refs[4:4 + N_MIX_PARAMS]
    pos = 4 + N_MIX_PARAMS
    hm_ref = refs[pos]
    state_out = refs[pos + 1:pos + 5]
    p_cur, p_next, gr_cur, gr_next, h_scr, upad, shf, dsc = refs[pos + 5:pos + 13]
    s_idx = pl.program_id(0)

    def hand_over():
        for jc in range(N_MIX // PROJ_TILE):
            cs = slice(jc * PROJ_TILE, (jc + 1) * PROJ_TILE)
            p_cur[:, cs] = p_next[:, cs]
        gr_cur[...] = gr_next[...]

    @pl.when(s_idx == 0)
    def _():
        _init_tables(prm[6], dsc, None, seq_len=seq_len, ch=ch, scale=HEAD_DIM ** -0.5)
        _project(x_ref, mod_ref, w_ref, wgt_ref, p_next, gr_next, h_scr, ch=ch)
        hand_over()

    @pl.when(s_idx > 0)
    def _():
        pieces = _project_pieces(x_ref, mod_ref, w_ref, wgt_ref, p_next, gr_next, h_scr, ch=ch)
        pieces[0]()
        _mix_context_sequence(p_cur, gr_cur, prm, hm_ref, state_out, upad, shf, dsc,
                              seq_len=seq_len, fillers=pieces[1:])
        hand_over()


def _mixer_context(x2d, mod, seq_len, layer, wp, wgt, params):
    rows = x2d.shape[0]
    bsz = rows // seq_len
    assert seq_len == SCAN_CHUNK, "context sequences are mixed as a single scan chunk"
    ch = seq_len
    n_ch = 1

    def nxt(s):
        return jnp.minimum(s, bsz - 1)

    def cur(s):
        return jnp.maximum(s - 1, 0)

    mat = pl.BlockSpec((None, 2, N_HEADS, HEAD_DIM, HEAD_DIM), lambda s: (cur(s), 0, 0, 0, 0))
    vec = pl.BlockSpec((None, 2 * N_HEADS, HEAD_DIM), lambda s: (cur(s), 0, 0))
    in_specs = (
        [pl.BlockSpec((seq_len, D_MODEL), lambda s: (nxt(s), 0)),
         pl.BlockSpec((None, 6, D_MODEL), lambda s: (0, 0, 0)),
         _weight_spec((D_MODEL, N_MIX), layer),
         _weight_spec((N_GATE_COLS, D_MODEL), layer)]
        + _mix_param_specs(ch, None))
    out_specs = [pl.BlockSpec((seq_len, 3 * BRANCH_W), lambda s: (cur(s), 0)), mat, vec, vec, mat]
    out_shape = [
        jax.ShapeDtypeStruct((rows, 3 * BRANCH_W), BF16),
        jax.ShapeDtypeStruct((bsz, 2, N_HEADS, HEAD_DIM, HEAD_DIM), F32),
        jax.ShapeDtypeStruct((bsz, 2 * N_HEADS, HEAD_DIM), F32),
        jax.ShapeDtypeStruct((bsz, 2 * N_HEADS, LANES), F32),
        jax.ShapeDtypeStruct((bsz, 2, N_HEADS, HEAD_DIM, HEAD_DIM), F32),
    ]
    scratch = [
        pltpu.VMEM((seq_len, N_MIX), BF16),
        pltpu.VMEM((seq_len, N_MIX), BF16),
        pltpu.VMEM((n_ch, N_GATE_COLS, ch), F32),
        pltpu.VMEM((n_ch, N_GATE_COLS, ch), F32),
        pltpu.VMEM((seq_len, D_MODEL), BF16),
        pltpu.VMEM((seq_len + 32, BRANCH_W), F32),
        pltpu.VMEM((SUBLANES, CONV_WIN, BRANCH_W), F32),
        pltpu.VMEM((N_HEADS, ch, ch), F32),
    ]
    return pl.pallas_call(
        functools.partial(_mixer_context_kernel, seq_len=seq_len, ch=ch, layer=layer),
        grid=(bsz + 1,),
        in_specs=in_specs,
        out_specs=out_specs,
        out_shape=out_shape,
        scratch_shapes=scratch,
        compiler_params=pltpu.CompilerParams(dimension_semantics=("arbitrary",)),
        name="mixer_context",
    )(x2d, mod, wp, wgt, *_mix_param_args(params))


def _post_kernel(hm_ref, x_ref, mod_ref, wgm_ref, wa_ref, wb_ref, wc_ref, wo_ref,
                 l1w_ref, l1b_ref, w13_ref, w2_ref, l2w_ref, l2b_ref, o_ref, *, alpha):
    shift1 = mod_ref[0:1, :]
    scale1 = mod_ref[1:2, :]
    gate1 = mod_ref[2:3, :]
    shift2 = mod_ref[3:4, :]
    scale2 = mod_ref[4:5, :]
    gate2 = mod_ref[5:6, :]
    x = x_ref[...]
    h1 = (x * (1.0 + scale1) + shift1).astype(BF16)
    merged = None
    for g, w_ref in enumerate((wa_ref, wb_ref, wc_ref)):
        gate = jax.nn.sigmoid(_dot(h1, wgm_ref[:, g * D_MODEL:(g + 1) * D_MODEL]))
        term = gate * _dot(hm_ref[:, g * BRANCH_W:(g + 1) * BRANCH_W], w_ref[...])
        merged = term if merged is None else merged + term
    mix = _dot(merged.astype(BF16), wo_ref[...])
    x1 = _layer_norm(alpha * x + gate1 * mix, l1w_ref[...], l1b_ref[...])
    h2 = (x1 * (1.0 + scale2) + shift2).astype(BF16)
    ff = jnp.zeros(x1.shape, F32)
    for c in range(FFN_HIDDEN // FFN_TILE):
        a = _dot(h2, w13_ref[:, c * FFN_TILE:(c + 1) * FFN_TILE])
        gt = _dot(h2, w13_ref[:, FFN_HIDDEN + c * FFN_TILE:FFN_HIDDEN + (c + 1) * FFN_TILE])
        act = (gt * jax.nn.sigmoid(gt)) * a
        ff = ff + _dot(act.astype(BF16), w2_ref[c * FFN_TILE:(c + 1) * FFN_TILE, :])
    o_ref[...] = _layer_norm(alpha * x1 + gate2 * ff, l2w_ref[...], l2b_ref[...])


def _post(hm, x2d, mod, seq_len, layer, w, alpha):
    rows = x2d.shape[0]
    tm = ROW_TILE
    n_seq = mod.shape[0]

    def seq_of(i):
        return (i * tm) // seq_len if n_seq > 1 else 0

    vec = _weight_spec((1, D_MODEL), layer)
    return pl.pallas_call(
        functools.partial(_post_kernel, alpha=alpha),
        grid=(rows // tm,),
        in_specs=[
            pl.BlockSpec((tm, 3 * BRANCH_W), lambda i: (i, 0)),
            pl.BlockSpec((tm, D_MODEL), lambda i: (i, 0)),
            pl.BlockSpec((None, 6, D_MODEL), lambda i: (seq_of(i), 0, 0)),
            _weight_spec((D_MODEL, N_GM), layer),
            _weight_spec((BRANCH_W, D_MODEL), layer), _weight_spec((BRANCH_W, D_MODEL), layer),
            _weight_spec((BRANCH_W, D_MODEL), layer),
            _weight_spec((D_MODEL, D_MODEL), layer), vec, vec,
            _weight_spec((D_MODEL, 2 * FFN_HIDDEN), layer), _weight_spec((FFN_HIDDEN, D_MODEL), layer), vec, vec,
        ],
        out_specs=pl.BlockSpec((tm, D_MODEL), lambda i: (i, 0)),
        out_shape=jax.ShapeDtypeStruct((rows, D_MODEL), F32),
        compiler_params=pltpu.CompilerParams(dimension_semantics=("parallel",)),
        name="post",
    )(hm, x2d, mod, w["gm_w"], w["mlstm_out_w"], w["conv_out_w"], w["ret_out_w"], w["out_w"],
      w["ln1_w"], w["ln1_b"], w["ffn_w13"], w["ffn_w2"], w["ln2_w"], w["ln2_b"])


def kernel(x_prompt, x_sample, state_mlstm_C, state_mlstm_n, state_mlstm_m, state_ret_S, c, c_ctx, ada_w, ada_b, in_w, mlstm_gate_b, mlstm_norm_w, mlstm_out_w, conv_w, conv_b, conv_ln_w, conv_ln_b, conv_out_w, ret_decay, ret_norm_w, ret_out_w, out_w, ln1_w, ln1_b, ln2_w, ln2_b, ffn_w13, ffn_w2):
    depth = in_w.shape[0]
    alpha = (2.0 * depth) ** 0.25
    bsz, seq, _ = x_prompt.shape
    dbsz, dseq, _ = x_sample.shape
    ch = SCAN_CHUNK
    gate_hi = GATE_LO + N_GATE_COLS

    wp = jnp.concatenate([in_w[:, :, :GATE_LO], in_w[:, :, gate_hi:gate_hi + N_MIX - GATE_LO]], axis=2).astype(BF16)
    wgt = jnp.swapaxes(in_w[:, :, GATE_LO:gate_hi].astype(BF16), 1, 2)
    dense = {
        "gm_w": in_w[:, :, gate_hi + N_MIX - GATE_LO:].astype(BF16),
        "mlstm_out_w": mlstm_out_w.astype(BF16), "conv_out_w": conv_out_w.astype(BF16),
        "ret_out_w": ret_out_w.astype(BF16), "out_w": out_w.astype(BF16),
        "ffn_w13": ffn_w13.astype(BF16), "ffn_w2": ffn_w2.astype(BF16),
        "ln1_w": ln1_w.reshape(depth, 1, D_MODEL), "ln1_b": ln1_b.reshape(depth, 1, D_MODEL),
        "ln2_w": ln2_w.reshape(depth, 1, D_MODEL), "ln2_b": ln2_b.reshape(depth, 1, D_MODEL),
    }
    gate_b_flat = mlstm_gate_b.reshape(depth, N_GATE_COLS)

    c_all = jnp.concatenate([c_ctx[None, :], c, jnp.zeros((8 - 1 - dbsz, D_MODEL), F32)], axis=0)
    mod = _ada_mod(c_all, ada_w, ada_b)

    y_prompt = x_prompt.reshape(bsz * seq, D_MODEL)
    y_sample = x_sample.reshape(dbsz * dseq, D_MODEL)
    new_c, new_n, new_m, new_s = [], [], [], []
    for l in range(depth):
        mix_params = {
            "gate_b_row": gate_b_flat[l][:, None],
            "mlstm_norm_w": mlstm_norm_w[l][None, :],
            "conv_w": jnp.pad(conv_w[l], ((0, 32 - CONV_K), (0, 0))),
            "conv_b": conv_b[l][None, :],
            "conv_ln_w": conv_ln_w[l][None, :],
            "conv_ln_b": conv_ln_b[l][None, :],
            "ret_decay": jnp.broadcast_to(ret_decay[l].reshape(2 * N_HEADS, 1), (2 * N_HEADS, ch)),
            "ret_norm_w": ret_norm_w[l][None, :],
        }
        mod_ctx = mod[l, 0:1].reshape(1, 6, D_MODEL)
        mod_lat = mod[l, 1:1 + dbsz].reshape(dbsz, 6, D_MODEL)

        hm, cn, nn, mn, sn = _mixer_context(y_prompt, mod_ctx, seq, l, wp, wgt, mix_params)
        y_prompt = _post(hm, y_prompt, mod_ctx, seq, l, dense, alpha)
        new_c.append(cn)
        new_n.append(nn.reshape(bsz, 2, N_HEADS, HEAD_DIM))
        new_m.append(mn[:, :, 0].reshape(bsz, 2, N_HEADS))
        new_s.append(sn)

        p, gr = _in_proj(y_sample, mod_lat, dseq, l, wp, wgt)
        hm = _seq_mix_latent(p, gr, dseq, l, mix_params,
                             (state_mlstm_C, state_mlstm_n, state_mlstm_m, state_ret_S))
        y_sample = _post(hm, y_sample, mod_lat, dseq, l, dense, alpha)

    return (y_prompt.reshape(bsz, seq, D_MODEL), y_sample.reshape(dbsz, dseq, D_MODEL),
            jnp.stack(new_c, 1), jnp.stack(new_n, 1), jnp.stack(new_m, 1), jnp.stack(new_s, 1))
```

```python
import functools
import math

import jax
import jax.numpy as jnp
from jax import lax
from jax.experimental import pallas as pl
from jax.experimental.pallas import tpu as pltpu

F32 = jnp.float32
BF16 = jnp.bfloat16

D_MODEL = 1024
N_HEADS = 4
HEAD_DIM = 128
BRANCH_W = N_HEADS * HEAD_DIM
N_GATE_COLS = 4 * N_HEADS
GATE_LO = 4 * BRANCH_W
CONV_K = 31
CONV_PAD = CONV_K // 2
FFN_HIDDEN = 2816
GRID_W = 64
ROPE_BASE = 10000.0
LN_EPS = 1e-5
N_MIX = 10 * BRANCH_W
N_GM = 3 * D_MODEL
LANES = 128
SUBLANES = 8

OFF_MQ, OFF_MK, OFF_MV, OFF_MO = 0, 512, 1024, 1536
OFF_CA, OFF_CG = 2048, 2560
OFF_RQ, OFF_RK, OFF_RV, OFF_RG = 3072, 3584, 4096, 4608

ROW_TILE = 512
PROJ_TILE = 1024
SCAN_CHUNK = 256
CONV_ROWS = 128
CONV_WIN = CONV_ROWS + 24
FFN_TILE = 256


def _dot(a, b):
    return jnp.dot(a, b, preferred_element_type=F32)


def _dot_nt(a, b):
    return lax.dot_general(a, b, (((1,), (1,)), ((), ())), preferred_element_type=F32)


def _dot_tn(a, b):
    return lax.dot_general(a, b, (((0,), (0,)), ((), ())), preferred_element_type=F32)


def _layer_norm(z, w, b):
    mu = jnp.mean(z, axis=-1, keepdims=True)
    zc = z - mu
    var = jnp.mean(zc * zc, axis=-1, keepdims=True)
    return zc * lax.rsqrt(var + LN_EPS) * w + b


def _head_norm(y, w):
    mu = jnp.mean(y, axis=-1, keepdims=True)
    yc = y - mu
    var = jnp.mean(yc * yc, axis=-1, keepdims=True)
    return yc * lax.rsqrt(var + LN_EPS) * w


def _log_sigmoid(x):
    return jnp.minimum(x, 0.0) - jnp.log1p(jnp.exp(-jnp.abs(x)))


def _split3(x):
    hi = x.astype(BF16)
    r = x - hi.astype(F32)
    mid = r.astype(BF16)
    lo = (r - mid.astype(F32)).astype(BF16)
    return hi, mid, lo


def _exact_right(x, sel):
    hi, mid, lo = _split3(x)
    return _dot(hi, sel) + _dot(mid, sel) + _dot(lo, sel)


def _exact_transpose(x, sel):
    hi, mid, lo = _split3(x)
    return _dot_tn(hi, sel) + _dot_tn(mid, sel) + _dot_tn(lo, sel)


def _ada_kernel(c_ref, w_ref, b_ref, o_ref):
    c = c_ref[...]
    s = c * jax.nn.sigmoid(c)
    o_ref[...] = _dot(s.astype(BF16), w_ref[...].astype(BF16)) + b_ref[...]


def _ada_mod(c_all, ada_w, ada_b):
    depth, _, n = ada_w.shape
    tn = 1024
    return pl.pallas_call(
        _ada_kernel,
        grid=(depth, n // tn),
        in_specs=[
            pl.BlockSpec((8, D_MODEL), lambda l, j: (0, 0)),
            pl.BlockSpec((None, D_MODEL, tn), lambda l, j: (l, 0, j)),
            pl.BlockSpec((None, 1, tn), lambda l, j: (l, 0, j)),
        ],
        out_specs=pl.BlockSpec((None, 8, tn), lambda l, j: (l, 0, j)),
        out_shape=jax.ShapeDtypeStruct((depth, 8, n), F32),
        name="ada_mod",
    )(c_all, ada_w, ada_b.reshape(depth, 1, n))


def _split_in_w_kernel(w_ref, wp_ref, gm_ref):
    gate_hi = GATE_LO + N_GATE_COLS
    gm_lo = gate_hi + N_MIX - GATE_LO
    wp_ref[:, 0:GATE_LO] = w_ref[:, 0:GATE_LO].astype(BF16)
    wp_ref[:, GATE_LO:N_MIX] = w_ref[:, gate_hi:gm_lo].astype(BF16)
    gm_ref[...] = w_ref[:, gm_lo:gm_lo + N_GM].astype(BF16)


def _split_in_w(in_w):
    depth, d_in, n_in = in_w.shape
    tr = 128
    return pl.pallas_call(
        _split_in_w_kernel,
        grid=(depth, d_in // tr),
        in_specs=[pl.BlockSpec((None, tr, n_in), lambda l, i: (l, i, 0))],
        out_specs=[pl.BlockSpec((None, tr, N_MIX), lambda l, i: (l, i, 0)),
                   pl.BlockSpec((None, tr, N_GM), lambda l, i: (l, i, 0))],
        out_shape=[jax.ShapeDtypeStruct((depth, d_in, N_MIX), BF16),
                   jax.ShapeDtypeStruct((depth, d_in, N_GM), BF16)],
        name="split_in_w",
    )(in_w)


def _project_pieces(x_ref, mod_ref, w_ref, wgt_ref, p_out, gr_out, h_scr, *, ch):
    n_rows = x_ref.shape[0]

    def modulate():
        shift = mod_ref[0:1, :]
        scale = mod_ref[1:2, :]
        h_scr[...] = (x_ref[...] * (1.0 + scale) + shift).astype(BF16)
        for cc in range(n_rows // ch):
            gr_out[cc] = _dot_nt(wgt_ref[...], h_scr[cc * ch:(cc + 1) * ch, :])

    def column_tile(jc):
        cs = slice(jc * PROJ_TILE, (jc + 1) * PROJ_TILE)
        p_out[:, cs] = _dot(h_scr[...], w_ref[:, cs]).astype(BF16)

    return [modulate] + [functools.partial(column_tile, jc) for jc in range(N_MIX // PROJ_TILE)]


def _project(x_ref, mod_ref, w_ref, wgt_ref, p_out, gr_out, h_scr, *, ch):
    for piece in _project_pieces(x_ref, mod_ref, w_ref, wgt_ref, p_out, gr_out, h_scr, ch=ch):
        piece()


def _inproj_kernel(x_ref, mod_ref, w_ref, wgt_ref, p_ref, gr_ref, h_scr, *, ch):
    _project(x_ref, mod_ref, w_ref, wgt_ref, p_ref, gr_ref, h_scr, ch=ch)


def _weight_spec(shape, layer):
    nd = len(shape)
    return pl.BlockSpec((None,) + shape, lambda *_: (layer,) + (0,) * nd, pipeline_mode=pl.Buffered(1))


def _in_proj(x2d, mod, seq_len, layer, wp, wgt):
    rows = x2d.shape[0]
    tm, ch = ROW_TILE, SCAN_CHUNK
    n_seq = mod.shape[0]

    def seq_of(i):
        return (i * tm) // seq_len if n_seq > 1 else 0

    return pl.pallas_call(
        functools.partial(_inproj_kernel, ch=ch),
        grid=(rows // tm,),
        in_specs=[
            pl.BlockSpec((tm, D_MODEL), lambda i: (i, 0)),
            pl.BlockSpec((None, 6, D_MODEL), lambda i: (seq_of(i), 0, 0)),
            _weight_spec((D_MODEL, N_MIX), layer),
            _weight_spec((N_GATE_COLS, D_MODEL), layer),
        ],
        out_specs=[
            pl.BlockSpec((tm, N_MIX), lambda i: (i, 0)),
            pl.BlockSpec((tm // ch, N_GATE_COLS, ch), lambda i: (i, 0, 0)),
        ],
        out_shape=[
            jax.ShapeDtypeStruct((rows, N_MIX), BF16),
            jax.ShapeDtypeStruct((rows // ch, N_GATE_COLS, ch), F32),
        ],
        scratch_shapes=[pltpu.VMEM((tm, D_MODEL), BF16)],
        compiler_params=pltpu.CompilerParams(dimension_semantics=("parallel",)),
        name="in_proj",
    )(x2d, mod, wp, wgt)


def _init_tables(dec_ref, dsc, rope_tabs, *, seq_len, ch, scale=None):
    hd = HEAD_DIM
    row_i = lax.broadcasted_iota(jnp.int32, (ch, ch), 0)
    col_i = lax.broadcasted_iota(jnp.int32, (ch, ch), 1)
    diff = (row_i - col_i).astype(F32)
    for h in range(N_HEADS):
        lg_f = _log_sigmoid(dec_ref[h:h + 1, :])
        lg_b = _log_sigmoid(dec_ref[N_HEADS + h:N_HEADS + h + 1, :])
        decay = (jnp.where(col_i <= row_i, jnp.exp(lg_f * jnp.maximum(diff, 0.0)), 0.0)
                 + jnp.where(col_i >= row_i, jnp.exp(lg_b * jnp.maximum(-diff, 0.0)), 0.0))
        dsc[h] = decay if scale is None else decay * scale
    if rope_tabs is not None:
        cos_t, sin_t = rope_tabs
        t_i = lax.broadcasted_iota(jnp.int32, (seq_len, hd), 0)
        lane = lax.broadcasted_iota(jnp.int32, (seq_len, hd), 1)
        pair = lane >> 1
        n_pairs = hd // 4
        freq = jnp.exp((pair & (n_pairs - 1)).astype(F32) * (-math.log(ROPE_BASE) / n_pairs))
        grid_pos = jnp.where(pair < n_pairs, t_i // GRID_W, t_i % GRID_W).astype(F32)
        ang = grid_pos * freq
        sin = jnp.sin(ang)
        cos_t[...] = jnp.cos(ang)
        sin_t[...] = jnp.where((lane & 1) == 0, -sin, sin)


def _conv_module(p, cw_ref, cb_ref, clw_ref, clb_ref, hm_ref, upad, shf, seq_len, emit_filler):
    zeros_pad = jnp.zeros((16, BRANCH_W), F32)
    upad[0:16, :] = zeros_pad
    upad[seq_len + 16:seq_len + 32, :] = zeros_pad
    for rb in range(seq_len // CONV_ROWS):
        r = slice(rb * CONV_ROWS, (rb + 1) * CONV_ROWS)
        ca = p[r, OFF_CA:OFF_CA + BRANCH_W].astype(F32)
        cg = p[r, OFF_CG:OFF_CG + BRANCH_W].astype(F32)
        upad[16 + rb * CONV_ROWS:16 + (rb + 1) * CONV_ROWS, :] = ca * jax.nn.sigmoid(cg)
    for rb in range(seq_len // CONV_ROWS):
        emit_filler()
        base = rb * CONV_ROWS
        for sft in range(1, SUBLANES):
            shf[sft] = upad[base + sft:base + sft + CONV_WIN, :]
        acc = None
        for k in range(CONV_K):
            a8, sft = divmod(k + 16 - CONV_PAD, SUBLANES)
            if sft == 0:
                win = upad[base + a8 * SUBLANES:base + a8 * SUBLANES + CONV_ROWS, :]
            else:
                win = shf[sft, a8 * SUBLANES:a8 * SUBLANES + CONV_ROWS, :]
            term = win * cw_ref[k:k + 1, :]
            acc = term if acc is None else acc + term
        acc = acc + cb_ref[...]
        u = _layer_norm(acc, clw_ref[...], clb_ref[...])
        u = u * jax.nn.sigmoid(u)
        hm_ref[rb * CONV_ROWS:(rb + 1) * CONV_ROWS, BRANCH_W:2 * BRANCH_W] = u.astype(BF16)


def _mix_context_sequence(p, gr, prm, hm_ref, state_out, upad, shf, dsc, *, seq_len, fillers=()):
    fillers = list(fillers)

    def emit_filler():
        if fillers:
            fillers.pop(0)()

    gbr_ref, mnw_ref, cw_ref, cb_ref, clw_ref, clb_ref, dec_ref, rnw_ref = prm
    cn_ref, nn_ref, mn_ref, sn_ref = state_out
    ch = seq_len
    hd = HEAD_DIM
    k_scale = HEAD_DIM ** -0.5
    log_k = math.log(k_scale)
    half = 2 * N_HEADS

    row_i = lax.broadcasted_iota(jnp.int32, (ch, ch), 0)
    col_i = lax.broadcasted_iota(jnp.int32, (ch, ch), 1)
    lower = col_i <= row_i
    upper = col_i >= row_i
    tri_l = lower.astype(BF16)
    tri_u = upper.astype(BF16)
    pos_col = lax.broadcasted_iota(jnp.int32, (ch, 1), 0).astype(F32)
    gate_lane = lax.broadcasted_iota(jnp.int32, (N_GATE_COLS, ch), 1)
    fwd_rows = lax.broadcasted_iota(jnp.int32, (N_GATE_COLS, ch), 0) < half
    ones_col = (lax.broadcasted_iota(jnp.int32, (ch, hd), 1) == 0).astype(BF16)
    n_stack = 3 * N_GATE_COLS
    eye_t = (lax.broadcasted_iota(jnp.int32, (n_stack, LANES), 0)
             == lax.broadcasted_iota(jnp.int32, (n_stack, LANES), 1)).astype(BF16)

    g = gr[0] + gbr_ref[...]
    lf_hi, lf_mid, lf_lo = _split3(_log_sigmoid(g))
    prefix = _dot(lf_hi, tri_u) + _dot(lf_mid, tri_u) + _dot(lf_lo, tri_u)
    suffix = _dot(lf_hi, tri_l) + _dot(lf_mid, tri_l) + _dot(lf_lo, tri_l)
    cum = jnp.where(fwd_rows, prefix, suffix)
    b_rows = pltpu.roll(cum, N_GATE_COLS - N_HEADS, 0)
    a_all = g - b_rows
    cm_f = a_all
    cm_b = a_all
    sft = 1
    while sft < ch:
        cm_f = jnp.maximum(cm_f, jnp.where(gate_lane >= sft, pltpu.roll(cm_f, sft, 1), -jnp.inf))
        cm_b = jnp.maximum(cm_b, jnp.where(gate_lane < ch - sft, pltpu.roll(cm_b, ch - sft, 1), -jnp.inf))
        sft *= 2
    big_m = jnp.maximum(jnp.where(fwd_rows, cm_f, cm_b), 0.0)
    m_true = b_rows + big_m
    cols = _exact_transpose(jnp.concatenate([a_all, big_m, m_true], axis=0), eye_t)
    edges = (ch - 1, 0)
    m_last = [big_m[half * d:half * d + N_HEADS, edges[d]:edges[d] + 1] for d in range(2)]
    for d in range(2):
        m_new = cum[half * d + N_HEADS:half * (d + 1), edges[d]:edges[d] + 1] + m_last[d]
        mn_ref[d * N_HEADS:(d + 1) * N_HEADS, :] = jnp.broadcast_to(m_new, (N_HEADS, LANES))

    _conv_module(p, cw_ref, cb_ref, clw_ref, clb_ref, hm_ref, upad, shf, seq_len, emit_filler)

    for h in range(N_HEADS):
        emit_filler()
        hc = slice(h * hd, (h + 1) * hd)

        q = p[:, OFF_MQ + h * hd:OFF_MQ + (h + 1) * hd]
        k = p[:, OFF_MK + h * hd:OFF_MK + (h + 1) * hd]
        v = p[:, OFF_MV + h * hd:OFF_MV + (h + 1) * hd]
        v_aug = jnp.concatenate([v, ones_col], axis=1)
        qk = _dot_nt(q, k)
        kf = k.astype(F32)
        h_sum = None
        for d in range(2):
            gi = half * d + h
            mask = lower if d == 0 else upper
            a_r = a_all[gi:gi + 1, :] + log_k
            a_c = cols[:, gi:gi + 1]
            bm_c = cols[:, N_GATE_COLS + gi:N_GATE_COLS + gi + 1]
            mt_c = cols[:, 2 * N_GATE_COLS + gi:2 * N_GATE_COLS + gi + 1]
            s = qk * jnp.exp(jnp.where(mask, a_r - bm_c, -jnp.inf))
            nd = _dot(s.astype(BF16), v_aug)
            h_dir = nd[:, 0:hd] / jnp.maximum(jnp.abs(nd[:, hd:hd + 1]), jnp.exp(-mt_c))
            h_sum = h_dir if h_sum is None else h_sum + h_dir
            kw = kf * (jnp.exp(a_c - m_last[d][h:h + 1, :]) * k_scale)
            cn_ref[d, h] = _dot_tn(kw.astype(BF16), v)
            nn_ref[d * N_HEADS + h:d * N_HEADS + h + 1, :] = jnp.sum(kw, axis=0, keepdims=True)
        y = _head_norm(h_sum, mnw_ref[:, hc])
        o = p[:, OFF_MO + h * hd:OFF_MO + (h + 1) * hd].astype(F32)
        hm_ref[:, hc] = (y * jax.nn.sigmoid(o)).astype(BF16)

        rq = p[:, OFF_RQ + h * hd:OFF_RQ + (h + 1) * hd]
        rk = p[:, OFF_RK + h * hd:OFF_RK + (h + 1) * hd]
        rv = p[:, OFF_RV + h * hd:OFF_RV + (h + 1) * hd]
        scores = _dot_nt(rq, rk) * dsc[h]
        y = _head_norm(_dot(scores.astype(BF16), rv), rnw_ref[:, hc])
        gt = p[:, OFF_RG + h * hd:OFF_RG + (h + 1) * hd].astype(F32)
        hm_ref[:, 2 * BRANCH_W + h * hd:2 * BRANCH_W + (h + 1) * hd] = (y * (gt * jax.nn.sigmoid(gt))).astype(BF16)
        rkf = rk.astype(F32)
        for d in range(2):
            lg = _log_sigmoid(dec_ref[d * N_HEADS + h:d * N_HEADS + h + 1, 0:1])
            zeta = jnp.exp(lg * ((ch - 1.0) - pos_col)) if d == 0 else jnp.exp(lg * pos_col)
            sn_ref[d, h] = _dot_tn((rkf * (zeta * k_scale)).astype(BF16), rv)

    while fillers:
        emit_filler()


def _mix_sequence(p, gr, prm, state_in, hm_ref, state_out, scr, rope_tabs, *, seq_len, ch, layer, seq_idx,
                  fillers=()):
    fillers = list(fillers)

    def emit_filler():
        if fillers:
            fillers.pop(0)()

    gbr_ref, mnw_ref, cw_ref, cb_ref, clw_ref, clb_ref, dec_ref, rnw_ref = prm
    acc_a, acc_c, upad, shf, rqs, rks, cst, nst, mst, sst, dsc = scr
    has_state = state_in is not None
    if has_state:
        c0_ref, n0_ref, m0_ref, s0_ref = state_in
    else:
        cn_ref, nn_ref, mn_ref, sn_ref = state_out

    n_ch = seq_len // ch
    hd = HEAD_DIM
    k_scale = HEAD_DIM ** -0.5

    row_i = lax.broadcasted_iota(jnp.int32, (ch, ch), 0)
    col_i = lax.broadcasted_iota(jnp.int32, (ch, ch), 1)
    lower = col_i <= row_i
    upper = col_i >= row_i
    tri_l = lower.astype(BF16)
    tri_u = upper.astype(BF16)
    pos_col = lax.broadcasted_iota(jnp.int32, (ch, 1), 0).astype(F32)
    gate_lane = lax.broadcasted_iota(jnp.int32, (N_GATE_COLS, ch), 1)
    ones_col = (lax.broadcasted_iota(jnp.int32, (ch, hd), 1) == 0).astype(BF16)
    n_stack = 3 * N_GATE_COLS
    eye_t = (lax.broadcasted_iota(jnp.int32, (n_stack, LANES), 0)
             == lax.broadcasted_iota(jnp.int32, (n_stack, LANES), 1)).astype(BF16)

    _conv_module(p, cw_ref, cb_ref, clw_ref, clb_ref, hm_ref, upad, shf, seq_len, emit_filler)

    if rope_tabs is not None:
        cos_t, sin_t = rope_tabs
        even = (lax.broadcasted_iota(jnp.int32, (seq_len, hd), 1) & 1) == 0

        def rope(x):
            swapped = jnp.where(even, pltpu.roll(x, hd - 1, 1), pltpu.roll(x, 1, 1))
            return x * cos_t[...] + swapped * sin_t[...]
    else:
        def rope(x):
            return x

    for h in range(N_HEADS):
        hc = slice(h * hd, (h + 1) * hd)
        rqs[:, hc] = rope(p[:, OFF_RQ + h * hd:OFF_RQ + (h + 1) * hd].astype(F32)).astype(BF16)
        rk = rope(p[:, OFF_RK + h * hd:OFF_RK + (h + 1) * hd].astype(F32))
        rks[:, hc] = (rk * k_scale).astype(BF16)

    mst[...] = jnp.zeros(mst.shape, F32)

    for d in range(2):
        mask = lower if d == 0 else upper
        tri_row = tri_u if d == 0 else tri_l
        r0 = 2 * N_HEADS * d
        edge = ch - 1 if d == 0 else 0
        order = range(n_ch) if d == 0 else range(n_ch - 1, -1, -1)
        log_gamma = [_log_sigmoid(dec_ref[d * N_HEADS + h:d * N_HEADS + h + 1, 0:1]) for h in range(N_HEADS)]

        if has_state:
            n_hi, n_mid, n_lo = _split3(n0_ref[...])
            sel_row = lax.broadcasted_iota(jnp.int32, (2 * N_HEADS, LANES), 0)
            sel_lane = lax.broadcasted_iota(jnp.int32, (2 * N_HEADS, LANES), 1)
            for h in range(N_HEADS):
                sel = ((sel_row == d * N_HEADS + h) & (sel_lane == 0)).astype(BF16)
                cst[h, :, 0:hd] = c0_ref[d, h]
                cst[h, :, hd:2 * hd] = _dot_tn(n_hi, sel) + _dot_tn(n_mid, sel) + _dot_tn(n_lo, sel)
                sst[h] = s0_ref[d, h]
                mst[r0 + h:r0 + h + 1, :] = jnp.full((1, LANES), m0_ref[seq_idx, layer, d, h], F32)

        for step, c in enumerate(order):
            no_carry = (not has_state) and step == 0
            need_update = (step < n_ch - 1) or (not has_state)
            rows = slice(c * ch, (c + 1) * ch)

            g = gr[c] + gbr_ref[...]
            cum = _exact_right(_log_sigmoid(g), tri_row)
            b_rows = pltpu.roll(cum, N_GATE_COLS - N_HEADS, 0)
            a_all = g - b_rows
            cm = a_all
            sft = 1
            while sft < ch:
                if d == 0:
                    moved = jnp.where(gate_lane >= sft, pltpu.roll(cm, sft, 1), -jnp.inf)
                else:
                    moved = jnp.where(gate_lane < ch - sft, pltpu.roll(cm, ch - sft, 1), -jnp.inf)
                cm = jnp.maximum(cm, moved)
                sft *= 2
            m0_all = mst[:, 0:1]
            big_m = jnp.maximum(cm, m0_all)
            m_true = b_rows + big_m
            cols = _exact_transpose(jnp.concatenate([a_all, big_m, m_true], axis=0), eye_t)
            m_last4 = big_m[r0:r0 + N_HEADS, edge:edge + 1]
            m_new4 = cum[r0 + N_HEADS:r0 + 2 * N_HEADS, edge:edge + 1] + m_last4
            carry4 = jnp.exp(m0_all[r0:r0 + N_HEADS, :] - m_last4)

            for h in range(N_HEADS):
                emit_filler()
                hc = slice(h * hd, (h + 1) * hd)
                gi = r0 + h

                a_r = a_all[gi:gi + 1, :]
                a_c = cols[:, gi:gi + 1]
                bm_c = cols[:, N_GATE_COLS + gi:N_GATE_COLS + gi + 1]
                mt_c = cols[:, 2 * N_GATE_COLS + gi:2 * N_GATE_COLS + gi + 1]
                q = p[rows, OFF_MQ + h * hd:OFF_MQ + (h + 1) * hd]
                kf = p[rows, OFF_MK + h * hd:OFF_MK + (h + 1) * hd].astype(F32) * k_scale
                v = p[rows, OFF_MV + h * hd:OFF_MV + (h + 1) * hd]
                v_aug = jnp.concatenate([v, ones_col], axis=1)

                s = _dot_nt(q, kf.astype(BF16)) * jnp.exp(jnp.where(mask, a_r - bm_c, -jnp.inf))
                nd = _dot(s.astype(BF16), v_aug)
                if not no_carry:
                    c_aug = cst[h]
                    w_inter = jnp.exp(m0_all[gi:gi + 1, :] - bm_c)
                    nd = nd + _dot(q, c_aug.astype(BF16)) * w_inter
                h_dir = nd[:, 0:hd] / jnp.maximum(jnp.abs(nd[:, hd:hd + 1]), jnp.exp(-mt_c))

                if need_update:
                    kw = kf * jnp.exp(a_c - m_last4[h:h + 1, :])
                    c_new = _dot_tn(kw.astype(BF16), v_aug)
                    if not no_carry:
                        c_new = c_new + c_aug * carry4[h:h + 1, :]
                    cst[h] = c_new
                    if not has_state:
                        n_new = jnp.sum(kw, axis=0, keepdims=True)
                        if not no_carry:
                            n_new = n_new + nst[gi:gi + 1, :] * carry4[h:h + 1, :]
                        nst[gi:gi + 1, :] = n_new

                if d == 0:
                    acc_a[rows, hc] = h_dir
                else:
                    y = _head_norm(acc_a[rows, hc] + h_dir, mnw_ref[:, hc])
                    o = p[rows, OFF_MO + h * hd:OFF_MO + (h + 1) * hd].astype(F32)
                    hm_ref[rows, hc] = (y * jax.nn.sigmoid(o)).astype(BF16)

                lg = log_gamma[h]
                rq = rqs[rows, hc]
                rk = rks[rows, hc]
                rv = p[rows, OFF_RV + h * hd:OFF_RV + (h + 1) * hd]
                if d == 0:
                    scores = _dot_nt(rq, rk) * dsc[h]
                    y_dir = _dot(scores.astype(BF16), rv)
                    xi = jnp.exp(lg * (pos_col + 1.0))
                    zeta = jnp.exp(lg * ((ch - 1.0) - pos_col))
                else:
                    y_dir = None
                    xi = jnp.exp(lg * (ch - pos_col))
                    zeta = jnp.exp(lg * pos_col)
                if not no_carry:
                    s0 = sst[h]
                    y_int = _dot(rq, s0.astype(BF16)) * xi
                    y_dir = y_int if y_dir is None else y_dir + y_int
                if need_update:
                    s_new = _dot_tn((rk.astype(F32) * zeta).astype(BF16), rv)
                    if not no_carry:
                        s_new = s_new + s0 * jnp.exp(lg * float(ch))
                    sst[h] = s_new

                hcc = slice(2 * BRANCH_W + h * hd, 2 * BRANCH_W + (h + 1) * hd)
                if d == 0:
                    acc_c[rows, hc] = y_dir
                else:
                    y_sum = acc_c[rows, hc] if y_dir is None else acc_c[rows, hc] + y_dir
                    y = _head_norm(y_sum, rnw_ref[:, hc])
                    gt = p[rows, OFF_RG + h * hd:OFF_RG + (h + 1) * hd].astype(F32)
                    hm_ref[rows, hcc] = (y * (gt * jax.nn.sigmoid(gt))).astype(BF16)

            if need_update:
                mst[r0:r0 + N_HEADS, :] = jnp.broadcast_to(m_new4, (N_HEADS, LANES))

        if not has_state:
            for h in range(N_HEADS):
                cn_ref[d, h] = cst[h, :, 0:hd]
                sn_ref[d, h] = sst[h]
            nn_ref[d * N_HEADS:(d + 1) * N_HEADS, :] = nst[r0:r0 + N_HEADS, :]
            mn_ref[d * N_HEADS:(d + 1) * N_HEADS, :] = mst[r0:r0 + N_HEADS, :]

    while fillers:
        emit_filler()


def _mix_scratch(seq_len, ch):
    return [
        pltpu.VMEM((seq_len, BRANCH_W), F32),
        pltpu.VMEM((seq_len, BRANCH_W), F32),
        pltpu.VMEM((seq_len + 32, BRANCH_W), F32),
        pltpu.VMEM((SUBLANES, CONV_WIN, BRANCH_W), F32),
        pltpu.VMEM((seq_len, BRANCH_W), BF16),
        pltpu.VMEM((seq_len, BRANCH_W), BF16),
        pltpu.VMEM((N_HEADS, HEAD_DIM, 2 * HEAD_DIM), F32),
        pltpu.VMEM((N_GATE_COLS, HEAD_DIM), F32),
        pltpu.VMEM((N_GATE_COLS, LANES), F32),
        pltpu.VMEM((N_HEADS, HEAD_DIM, HEAD_DIM), F32),
        pltpu.VMEM((N_HEADS, ch, ch), F32),
    ]


N_MIX_PARAMS = 8
N_MIX_SCRATCH = 11


def _mix_param_specs(ch, index_map):
    def full(shape):
        return pl.BlockSpec(shape, lambda *a: (0,) * len(shape))

    del index_map
    return [full((N_GATE_COLS, 1)), full((1, BRANCH_W)), full((32, BRANCH_W)), full((1, BRANCH_W)),
            full((1, BRANCH_W)), full((1, BRANCH_W)), full((8, ch)), full((1, BRANCH_W))]


def _mix_param_args(params):
    return [params["gate_b_row"], params["mlstm_norm_w"], params["conv_w"], params["conv_b"],
            params["conv_ln_w"], params["conv_ln_b"], params["ret_decay"], params["ret_norm_w"]]


def _seqmix_latent_kernel(*refs, seq_len, ch, layer):
    p_ref, gr_ref = refs[:2]
    prm = refs[2:2 + N_MIX_PARAMS]
    pos = 2 + N_MIX_PARAMS
    state_in = refs[pos:pos + 4]
    hm_ref = refs[pos + 4]
    scr = refs[pos + 5:pos + 5 + N_MIX_SCRATCH]
    rope_tabs = refs[pos + 5 + N_MIX_SCRATCH:]
    b_idx = pl.program_id(0)

    @pl.when(b_idx == 0)
    def _():
        _init_tables(prm[6], scr[10], rope_tabs, seq_len=seq_len, ch=ch)

    _mix_sequence(p_ref, gr_ref, prm, state_in, hm_ref, None, scr, rope_tabs,
                  seq_len=seq_len, ch=ch, layer=layer, seq_idx=b_idx)


def _seq_mix_latent(p, gr, seq_len, layer, params, state):
    rows = p.shape[0]
    bsz = rows // seq_len
    ch = SCAN_CHUNK
    n_ch = seq_len // ch
    c0, n0, m0, s0 = state
    mat = pl.BlockSpec((None, None, 2, N_HEADS, HEAD_DIM, HEAD_DIM), lambda b: (b, layer, 0, 0, 0, 0))
    in_specs = (
        [pl.BlockSpec((seq_len, N_MIX), lambda b: (b, 0)),
         pl.BlockSpec((n_ch, N_GATE_COLS, ch), lambda b: (b, 0, 0))]
        + _mix_param_specs(ch, None)
        + [mat, pl.BlockSpec((None, None, 2 * N_HEADS, HEAD_DIM), lambda b: (b, layer, 0, 0)),
           pl.BlockSpec(memory_space=pltpu.SMEM), mat])
    args = [p, gr] + _mix_param_args(params) + [
        c0, n0.reshape(n0.shape[0], n0.shape[1], 2 * N_HEADS, HEAD_DIM), m0, s0]
    scratch = _mix_scratch(seq_len, ch) + [pltpu.VMEM((seq_len, HEAD_DIM), F32),
                                           pltpu.VMEM((seq_len, HEAD_DIM), F32)]
    return pl.pallas_call(
        functools.partial(_seqmix_latent_kernel, seq_len=seq_len, ch=ch, layer=layer),
        grid=(bsz,),
        in_specs=in_specs,
        out_specs=pl.BlockSpec((seq_len, 3 * BRANCH_W), lambda b: (b, 0)),
        out_shape=jax.ShapeDtypeStruct((rows, 3 * BRANCH_W), BF16),
        scratch_shapes=scratch,
        compiler_params=pltpu.CompilerParams(dimension_semantics=("arbitrary",)),
        name="seq_mix_latent",
    )(*args)


def _mixer_context_kernel(*refs, seq_len, ch, layer, n_carried):
    x_ref, mod_ref, w_ref, wgt_ref = refs[:4]
    prm = refs[4:4 + N_MIX_PARAMS]
    pos = 4 + N_MIX_PARAMS + n_carried
    hm_ref = refs[pos]
    state_out = refs[pos + 1:pos + 5]
    p_cur, p_next, gr_cur, gr_next, h_scr, upad, shf, dsc = refs[pos + 5:pos + 13]
    s_idx = pl.program_id(0)

    def hand_over():
        for jc in range(N_MIX // PROJ_TILE):
            cs = slice(jc * PROJ_TILE, (jc + 1) * PROJ_TILE)
            p_cur[:, cs] = p_next[:, cs]
        gr_cur[...] = gr_next[...]

    @pl.when(s_idx == 0)
    def _():
        _init_tables(prm[6], dsc, None, seq_len=seq_len, ch=ch, scale=HEAD_DIM ** -0.5)
        _project(x_ref, mod_ref, w_ref, wgt_ref, p_next, gr_next, h_scr, ch=ch)
        hand_over()

    @pl.when(s_idx > 0)
    def _():
        pieces = _project_pieces(x_ref, mod_ref, w_ref, wgt_ref, p_next, gr_next, h_scr, ch=ch)
        pieces[0]()
        _mix_context_sequence(p_cur, gr_cur, prm, hm_ref, state_out, upad, shf, dsc,
                              seq_len=seq_len, fillers=pieces[1:])
        hand_over()


def _mixer_context(x2d, mod, seq_len, layer, depth, wp, wgt, params, carried):
    rows = x2d.shape[0]
    bsz = rows // seq_len
    assert seq_len == SCAN_CHUNK, "context sequences are mixed as a single scan chunk"
    ch = seq_len
    n_ch = 1

    def nxt(s):
        return jnp.minimum(s, bsz - 1)

    def cur(s):
        return jnp.maximum(s - 1, 0)

    mat = pl.BlockSpec((None, None, 2, N_HEADS, HEAD_DIM, HEAD_DIM), lambda s: (cur(s), layer, 0, 0, 0, 0))
    vec = pl.BlockSpec((None, None, 2 * N_HEADS, HEAD_DIM), lambda s: (cur(s), layer, 0, 0))
    carried = [] if carried is None else list(carried)
    n_fixed = 4 + N_MIX_PARAMS
    in_specs = (
        [pl.BlockSpec((seq_len, D_MODEL), lambda s: (nxt(s), 0)),
         pl.BlockSpec((None, 6, D_MODEL), lambda s: (0, 0, 0)),
         _weight_spec((D_MODEL, N_MIX), layer),
         _weight_spec((N_GATE_COLS, D_MODEL), layer)]
        + _mix_param_specs(ch, None)
        + [pl.BlockSpec(memory_space=pl.ANY)] * len(carried))
    out_specs = [pl.BlockSpec((seq_len, 3 * BRANCH_W), lambda s: (cur(s), 0)), mat, vec, vec, mat]
    out_shape = [
        jax.ShapeDtypeStruct((rows, 3 * BRANCH_W), BF16),
        jax.ShapeDtypeStruct((bsz, depth, 2, N_HEADS, HEAD_DIM, HEAD_DIM), F32),
        jax.ShapeDtypeStruct((bsz, depth, 2 * N_HEADS, HEAD_DIM), F32),
        jax.ShapeDtypeStruct((bsz, depth, 2 * N_HEADS, LANES), F32),
        jax.ShapeDtypeStruct((bsz, depth, 2, N_HEADS, HEAD_DIM, HEAD_DIM), F32),
    ]
    scratch = [
        pltpu.VMEM((seq_len, N_MIX), BF16),
        pltpu.VMEM((seq_len, N_MIX), BF16),
        pltpu.VMEM((n_ch, N_GATE_COLS, ch), F32),
        pltpu.VMEM((n_ch, N_GATE_COLS, ch), F32),
        pltpu.VMEM((seq_len, D_MODEL), BF16),
        pltpu.VMEM((seq_len + 32, BRANCH_W), F32),
        pltpu.VMEM((SUBLANES, CONV_WIN, BRANCH_W), F32),
        pltpu.VMEM((N_HEADS, ch, ch), F32),
    ]
    return pl.pallas_call(
        functools.partial(_mixer_context_kernel, seq_len=seq_len, ch=ch, layer=layer, n_carried=len(carried)),
        grid=(bsz + 1,),
        in_specs=in_specs,
        out_specs=out_specs,
        out_shape=out_shape,
        scratch_shapes=scratch,
        input_output_aliases={n_fixed + i: 1 + i for i in range(len(carried))},
        compiler_params=pltpu.CompilerParams(dimension_semantics=("arbitrary",)),
        name="mixer_context",
    )(x2d, mod, wp, wgt, *_mix_param_args(params), *carried)


def _post_kernel(hm_ref, x_ref, mod_ref, wgm_ref, wa_ref, wb_ref, wc_ref, wo_ref,
                 l1w_ref, l1b_ref, w13_ref, w2_ref, l2w_ref, l2b_ref, o_ref, *, alpha):
    shift1 = mod_ref[0:1, :]
    scale1 = mod_ref[1:2, :]
    gate1 = mod_ref[2:3, :]
    shift2 = mod_ref[3:4, :]
    scale2 = mod_ref[4:5, :]
    gate2 = mod_ref[5:6, :]
    x = x_ref[...]
    h1 = (x * (1.0 + scale1) + shift1).astype(BF16)
    merged = None
    for g, w_ref in enumerate((wa_ref, wb_ref, wc_ref)):
        gate = jax.nn.sigmoid(_dot(h1, wgm_ref[:, g * D_MODEL:(g + 1) * D_MODEL]))
        term = gate * _dot(hm_ref[:, g * BRANCH_W:(g + 1) * BRANCH_W], w_ref[...])
        merged = term if merged is None else merged + term
    mix = _dot(merged.astype(BF16), wo_ref[...])
    x1 = _layer_norm(alpha * x + gate1 * mix, l1w_ref[...], l1b_ref[...])
    h2 = (x1 * (1.0 + scale2) + shift2).astype(BF16)
    ff = jnp.zeros(x1.shape, F32)
    for c in range(FFN_HIDDEN // FFN_TILE):
        a = _dot(h2, w13_ref[:, c * FFN_TILE:(c + 1) * FFN_TILE])
        gt = _dot(h2, w13_ref[:, FFN_HIDDEN + c * FFN_TILE:FFN_HIDDEN + (c + 1) * FFN_TILE])
        act = (gt * jax.nn.sigmoid(gt)) * a
        ff = ff + _dot(act.astype(BF16), w2_ref[c * FFN_TILE:(c + 1) * FFN_TILE, :])
    o_ref[...] = _layer_norm(alpha * x1 + gate2 * ff, l2w_ref[...], l2b_ref[...])


def _post(hm, x2d, mod, seq_len, layer, w, alpha):
    rows = x2d.shape[0]
    tm = ROW_TILE
    n_seq = mod.shape[0]

    def seq_of(i):
        return (i * tm) // seq_len if n_seq > 1 else 0

    vec = _weight_spec((1, D_MODEL), layer)
    return pl.pallas_call(
        functools.partial(_post_kernel, alpha=alpha),
        grid=(rows // tm,),
        in_specs=[
            pl.BlockSpec((tm, 3 * BRANCH_W), lambda i: (i, 0)),
            pl.BlockSpec((tm, D_MODEL), lambda i: (i, 0)),
            pl.BlockSpec((None, 6, D_MODEL), lambda i: (seq_of(i), 0, 0)),
            _weight_spec((D_MODEL, N_GM), layer),
            _weight_spec((BRANCH_W, D_MODEL), layer), _weight_spec((BRANCH_W, D_MODEL), layer),
            _weight_spec((BRANCH_W, D_MODEL), layer),
            _weight_spec((D_MODEL, D_MODEL), layer), vec, vec,
            _weight_spec((D_MODEL, 2 * FFN_HIDDEN), layer), _weight_spec((FFN_HIDDEN, D_MODEL), layer), vec, vec,
        ],
        out_specs=pl.BlockSpec((tm, D_MODEL), lambda i: (i, 0)),
        out_shape=jax.ShapeDtypeStruct((rows, D_MODEL), F32),
        compiler_params=pltpu.CompilerParams(dimension_semantics=("parallel",)),
        name="post",
    )(hm, x2d, mod, w["gm_w"], w["mlstm_out_w"], w["conv_out_w"], w["ret_out_w"], w["out_w"],
      w["ln1_w"], w["ln1_b"], w["ffn_w13"], w["ffn_w2"], w["ln2_w"], w["ln2_b"])


def kernel(x_prompt, x_sample, state_mlstm_C, state_mlstm_n, state_mlstm_m, state_ret_S, c, c_ctx, ada_w, ada_b, in_w, mlstm_gate_b, mlstm_norm_w, mlstm_out_w, conv_w, conv_b, conv_ln_w, conv_ln_b, conv_out_w, ret_decay, ret_norm_w, ret_out_w, out_w, ln1_w, ln1_b, ln2_w, ln2_b, ffn_w13, ffn_w2):
    depth = in_w.shape[0]
    alpha = (2.0 * depth) ** 0.25
    bsz, seq, _ = x_prompt.shape
    dbsz, dseq, _ = x_sample.shape
    ch = SCAN_CHUNK
    gate_hi = GATE_LO + N_GATE_COLS

    wp, gm_w = _split_in_w(in_w)
    wgt = jnp.swapaxes(in_w[:, :, GATE_LO:gate_hi].astype(BF16), 1, 2)
    dense = {
        "gm_w": gm_w,
        "mlstm_out_w": mlstm_out_w.astype(BF16), "conv_out_w": conv_out_w.astype(BF16),
        "ret_out_w": ret_out_w.astype(BF16), "out_w": out_w.astype(BF16),
        "ffn_w13": ffn_w13.astype(BF16), "ffn_w2": ffn_w2.astype(BF16),
        "ln1_w": ln1_w.reshape(depth, 1, D_MODEL), "ln1_b": ln1_b.reshape(depth, 1, D_MODEL),
        "ln2_w": ln2_w.reshape(depth, 1, D_MODEL), "ln2_b": ln2_b.reshape(depth, 1, D_MODEL),
    }
    gate_b_flat = mlstm_gate_b.reshape(depth, N_GATE_COLS)

    c_all = jnp.concatenate([c_ctx[None, :], c, jnp.zeros((8 - 1 - dbsz, D_MODEL), F32)], axis=0)
    mod = _ada_mod(c_all, ada_w, ada_b)

    y_prompt = x_prompt.reshape(bsz * seq, D_MODEL)
    y_sample = x_sample.reshape(dbsz * dseq, D_MODEL)
    states = None
    for l in range(depth):
        mix_params = {
            "gate_b_row": gate_b_flat[l][:, None],
            "mlstm_norm_w": mlstm_norm_w[l][None, :],
            "conv_w": jnp.pad(conv_w[l], ((0, 32 - CONV_K), (0, 0))),
            "conv_b": conv_b[l][None, :],
            "conv_ln_w": conv_ln_w[l][None, :],
            "conv_ln_b": conv_ln_b[l][None, :],
            "ret_decay": jnp.broadcast_to(ret_decay[l].reshape(2 * N_HEADS, 1), (2 * N_HEADS, ch)),
            "ret_norm_w": ret_norm_w[l][None, :],
        }
        mod_ctx = mod[l, 0:1].reshape(1, 6, D_MODEL)
        mod_lat = mod[l, 1:1 + dbsz].reshape(dbsz, 6, D_MODEL)

        hm, *states = _mixer_context(y_prompt, mod_ctx, seq, l, depth, wp, wgt, mix_params, states)
        y_prompt = _post(hm, y_prompt, mod_ctx, seq, l, dense, alpha)

        p, gr = _in_proj(y_sample, mod_lat, dseq, l, wp, wgt)
        hm = _seq_mix_latent(p, gr, dseq, l, mix_params,
                             (state_mlstm_C, state_mlstm_n, state_mlstm_m, state_ret_S))
        y_sample = _post(hm, y_sample, mod_lat, dseq, l, dense, alpha)

    new_c, new_n, new_m, new_s = states
    return (y_prompt.reshape(bsz, seq, D_MODEL), y_sample.reshape(dbsz, dseq, D_MODEL),
            new_c, new_n.reshape(bsz, depth, 2, N_HEADS, HEAD_DIM),
            new_m[:, :, :, 0].reshape(bsz, depth, 2, N_HEADS), new_s)
```

```python
import functools
import math

import jax
import jax.numpy as jnp
from jax import lax
from jax.experimental import pallas as pl
from jax.experimental.pallas import tpu as pltpu

F32 = jnp.float32
BF16 = jnp.bfloat16

D_MODEL = 1024
N_HEADS = 4
HEAD_DIM = 128
BRANCH_W = N_HEADS * HEAD_DIM
N_GATE_COLS = 4 * N_HEADS
GATE_LO = 4 * BRANCH_W
CONV_K = 31
CONV_PAD = CONV_K // 2
FFN_HIDDEN = 2816
GRID_W = 64
ROPE_BASE = 10000.0
LN_EPS = 1e-5
N_MIX = 10 * BRANCH_W
N_GM = 3 * D_MODEL
LANES = 128
SUBLANES = 8

OFF_MQ, OFF_MK, OFF_MV, OFF_MO = 0, 512, 1024, 1536
OFF_CA, OFF_CG = 2048, 2560
OFF_RQ, OFF_RK, OFF_RV, OFF_RG = 3072, 3584, 4096, 4608

ROW_TILE = 512
PROJ_TILE = 1024
SCAN_CHUNK = 256
CONV_ROWS = 128
CONV_WIN = CONV_ROWS + 24
FFN_TILE = 256


def _dot(a, b):
    return jnp.dot(a, b, preferred_element_type=F32)


def _dot_nt(a, b):
    return lax.dot_general(a, b, (((1,), (1,)), ((), ())), preferred_element_type=F32)


def _dot_tn(a, b):
    return lax.dot_general(a, b, (((0,), (0,)), ((), ())), preferred_element_type=F32)


def _layer_norm(z, w, b):
    mu = jnp.mean(z, axis=-1, keepdims=True)
    zc = z - mu
    var = jnp.mean(zc * zc, axis=-1, keepdims=True)
    return zc * lax.rsqrt(var + LN_EPS) * w + b


def _head_norm(y, w):
    mu = jnp.mean(y, axis=-1, keepdims=True)
    yc = y - mu
    var = jnp.mean(yc * yc, axis=-1, keepdims=True)
    return yc * lax.rsqrt(var + LN_EPS) * w


def _log_sigmoid(x):
    return jnp.minimum(x, 0.0) - jnp.log1p(jnp.exp(-jnp.abs(x)))


def _split3(x):
    hi = x.astype(BF16)
    r = x - hi.astype(F32)
    mid = r.astype(BF16)
    lo = (r - mid.astype(F32)).astype(BF16)
    return hi, mid, lo


def _exact_right(x, sel):
    hi, mid, lo = _split3(x)
    return _dot(hi, sel) + _dot(mid, sel) + _dot(lo, sel)


def _exact_transpose(x, sel):
    hi, mid, lo = _split3(x)
    return _dot_tn(hi, sel) + _dot_tn(mid, sel) + _dot_tn(lo, sel)


def _ada_kernel(c_ref, w_ref, b_ref, o_ref):
    c = c_ref[...]
    s = c * jax.nn.sigmoid(c)
    o_ref[...] = _dot(s.astype(BF16), w_ref[...].astype(BF16)) + b_ref[...]


def _ada_mod(c_all, ada_w, ada_b):
    depth, _, n = ada_w.shape
    tn = 1024
    return pl.pallas_call(
        _ada_kernel,
        grid=(depth, n // tn),
        in_specs=[
            pl.BlockSpec((8, D_MODEL), lambda l, j: (0, 0)),
            pl.BlockSpec((None, D_MODEL, tn), lambda l, j: (l, 0, j)),
            pl.BlockSpec((None, 1, tn), lambda l, j: (l, 0, j)),
        ],
        out_specs=pl.BlockSpec((None, 8, tn), lambda l, j: (l, 0, j)),
        out_shape=jax.ShapeDtypeStruct((depth, 8, n), F32),
        name="ada_mod",
    )(c_all, ada_w, ada_b.reshape(depth, 1, n))


def _split_in_w_kernel(w_ref, wp_ref, gm_ref, wgt_ref):
    gate_hi = GATE_LO + N_GATE_COLS
    gm_lo = gate_hi + N_MIX - GATE_LO
    wp_ref[:, 0:GATE_LO] = w_ref[:, 0:GATE_LO].astype(BF16)
    wp_ref[:, GATE_LO:N_MIX] = w_ref[:, gate_hi:gm_lo].astype(BF16)
    gm_ref[...] = w_ref[:, gm_lo:gm_lo + N_GM].astype(BF16)
    wgt_ref[...] = w_ref[:, GATE_LO:GATE_LO + LANES].T[0:N_GATE_COLS, :].astype(BF16)


def _split_in_w(in_w):
    depth, d_in, n_in = in_w.shape
    tr = 128
    return pl.pallas_call(
        _split_in_w_kernel,
        grid=(depth, d_in // tr),
        in_specs=[pl.BlockSpec((None, tr, n_in), lambda l, i: (l, i, 0))],
        out_specs=[pl.BlockSpec((None, tr, N_MIX), lambda l, i: (l, i, 0)),
                   pl.BlockSpec((None, tr, N_GM), lambda l, i: (l, i, 0)),
                   pl.BlockSpec((None, N_GATE_COLS, tr), lambda l, i: (l, 0, i))],
        out_shape=[jax.ShapeDtypeStruct((depth, d_in, N_MIX), BF16),
                   jax.ShapeDtypeStruct((depth, d_in, N_GM), BF16),
                   jax.ShapeDtypeStruct((depth, N_GATE_COLS, d_in), BF16)],
        name="split_in_w",
    )(in_w)


def _project_pieces(x_ref, mod_ref, w_ref, wgt_ref, p_out, gr_out, h_scr, *, ch):
    n_rows = x_ref.shape[0]

    def modulate():
        shift = mod_ref[0:1, :]
        scale = mod_ref[1:2, :]
        h_scr[...] = (x_ref[...] * (1.0 + scale) + shift).astype(BF16)
        for cc in range(n_rows // ch):
            gr_out[cc] = _dot_nt(wgt_ref[...], h_scr[cc * ch:(cc + 1) * ch, :])

    def column_tile(jc):
        cs = slice(jc * PROJ_TILE, (jc + 1) * PROJ_TILE)
        p_out[:, cs] = _dot(h_scr[...], w_ref[:, cs]).astype(BF16)

    return [modulate] + [functools.partial(column_tile, jc) for jc in range(N_MIX // PROJ_TILE)]


def _project(x_ref, mod_ref, w_ref, wgt_ref, p_out, gr_out, h_scr, *, ch):
    for piece in _project_pieces(x_ref, mod_ref, w_ref, wgt_ref, p_out, gr_out, h_scr, ch=ch):
        piece()


def _inproj_kernel(x_ref, mod_ref, w_ref, wgt_ref, p_ref, gr_ref, h_scr, *, ch):
    _project(x_ref, mod_ref, w_ref, wgt_ref, p_ref, gr_ref, h_scr, ch=ch)


def _weight_spec(shape, layer):
    nd = len(shape)
    return pl.BlockSpec((None,) + shape, lambda *_: (layer,) + (0,) * nd, pipeline_mode=pl.Buffered(1))


def _in_proj(x2d, mod, seq_len, layer, wp, wgt):
    rows = x2d.shape[0]
    tm, ch = ROW_TILE, SCAN_CHUNK
    n_seq = mod.shape[0]

    def seq_of(i):
        return (i * tm) // seq_len if n_seq > 1 else 0

    return pl.pallas_call(
        functools.partial(_inproj_kernel, ch=ch),
        grid=(rows // tm,),
        in_specs=[
            pl.BlockSpec((tm, D_MODEL), lambda i: (i, 0)),
            pl.BlockSpec((None, 6, D_MODEL), lambda i: (seq_of(i), 0, 0)),
            _weight_spec((D_MODEL, N_MIX), layer),
            _weight_spec((N_GATE_COLS, D_MODEL), layer),
        ],
        out_specs=[
            pl.BlockSpec((tm, N_MIX), lambda i: (i, 0)),
            pl.BlockSpec((tm // ch, N_GATE_COLS, ch), lambda i: (i, 0, 0)),
        ],
        out_shape=[
            jax.ShapeDtypeStruct((rows, N_MIX), BF16),
            jax.ShapeDtypeStruct((rows // ch, N_GATE_COLS, ch), F32),
        ],
        scratch_shapes=[pltpu.VMEM((tm, D_MODEL), BF16)],
        compiler_params=pltpu.CompilerParams(dimension_semantics=("parallel",)),
        name="in_proj",
    )(x2d, mod, wp, wgt)


def _init_tables(dec_ref, dsc, rope_tabs, *, seq_len, ch, scale=None):
    hd = HEAD_DIM
    row_i = lax.broadcasted_iota(jnp.int32, (ch, ch), 0)
    col_i = lax.broadcasted_iota(jnp.int32, (ch, ch), 1)
    diff = (row_i - col_i).astype(F32)
    for h in range(N_HEADS):
        lg_f = _log_sigmoid(dec_ref[h:h + 1, :])
        lg_b = _log_sigmoid(dec_ref[N_HEADS + h:N_HEADS + h + 1, :])
        decay = (jnp.where(col_i <= row_i, jnp.exp(lg_f * jnp.maximum(diff, 0.0)), 0.0)
                 + jnp.where(col_i >= row_i, jnp.exp(lg_b * jnp.maximum(-diff, 0.0)), 0.0))
        dsc[h] = decay if scale is None else decay * scale
    if rope_tabs is not None:
        cos_t, sin_t = rope_tabs
        t_i = lax.broadcasted_iota(jnp.int32, (seq_len, hd), 0)
        lane = lax.broadcasted_iota(jnp.int32, (seq_len, hd), 1)
        pair = lane >> 1
        n_pairs = hd // 4
        freq = jnp.exp((pair & (n_pairs - 1)).astype(F32) * (-math.log(ROPE_BASE) / n_pairs))
        grid_pos = jnp.where(pair < n_pairs, t_i // GRID_W, t_i % GRID_W).astype(F32)
        ang = grid_pos * freq
        sin = jnp.sin(ang)
        cos_t[...] = jnp.cos(ang)
        sin_t[...] = jnp.where((lane & 1) == 0, -sin, sin)


def _conv_module(p, cw_ref, cb_ref, clw_ref, clb_ref, hm_ref, upad, shf, seq_len, emit_filler):
    zeros_pad = jnp.zeros((16, BRANCH_W), F32)
    upad[0:16, :] = zeros_pad
    upad[seq_len + 16:seq_len + 32, :] = zeros_pad
    for rb in range(seq_len // CONV_ROWS):
        r = slice(rb * CONV_ROWS, (rb + 1) * CONV_ROWS)
        ca = p[r, OFF_CA:OFF_CA + BRANCH_W].astype(F32)
        cg = p[r, OFF_CG:OFF_CG + BRANCH_W].astype(F32)
        upad[16 + rb * CONV_ROWS:16 + (rb + 1) * CONV_ROWS, :] = ca * jax.nn.sigmoid(cg)
    for rb in range(seq_len // CONV_ROWS):
        emit_filler()
        base = rb * CONV_ROWS
        for sft in range(1, SUBLANES):
            shf[sft] = upad[base + sft:base + sft + CONV_WIN, :]
        acc = None
        for k in range(CONV_K):
            a8, sft = divmod(k + 16 - CONV_PAD, SUBLANES)
            if sft == 0:
                win = upad[base + a8 * SUBLANES:base + a8 * SUBLANES + CONV_ROWS, :]
            else:
                win = shf[sft, a8 * SUBLANES:a8 * SUBLANES + CONV_ROWS, :]
            term = win * cw_ref[k:k + 1, :]
            acc = term if acc is None else acc + term
        acc = acc + cb_ref[...]
        u = _layer_norm(acc, clw_ref[...], clb_ref[...])
        u = u * jax.nn.sigmoid(u)
        hm_ref[rb * CONV_ROWS:(rb + 1) * CONV_ROWS, BRANCH_W:2 * BRANCH_W] = u.astype(BF16)


def _mix_context_sequence(p, gr, prm, hm_ref, state_out, upad, shf, dsc, *, seq_len, fillers=()):
    fillers = list(fillers)

    def emit_filler():
        if fillers:
            fillers.pop(0)()

    gbr_ref, mnw_ref, cw_ref, cb_ref, clw_ref, clb_ref, dec_ref, rnw_ref = prm
    cn_ref, nn_ref, mn_ref, sn_ref = state_out
    ch = seq_len
    hd = HEAD_DIM
    k_scale = HEAD_DIM ** -0.5
    log_k = math.log(k_scale)
    half = 2 * N_HEADS

    row_i = lax.broadcasted_iota(jnp.int32, (ch, ch), 0)
    col_i = lax.broadcasted_iota(jnp.int32, (ch, ch), 1)
    lower = col_i <= row_i
    upper = col_i >= row_i
    tri_l = lower.astype(BF16)
    tri_u = upper.astype(BF16)
    pos_col = lax.broadcasted_iota(jnp.int32, (ch, 1), 0).astype(F32)
    gate_lane = lax.broadcasted_iota(jnp.int32, (N_GATE_COLS, ch), 1)
    fwd_rows = lax.broadcasted_iota(jnp.int32, (N_GATE_COLS, ch), 0) < half
    ones_col = (lax.broadcasted_iota(jnp.int32, (ch, hd), 1) == 0).astype(BF16)
    n_stack = 3 * N_GATE_COLS
    eye_t = (lax.broadcasted_iota(jnp.int32, (n_stack, LANES), 0)
             == lax.broadcasted_iota(jnp.int32, (n_stack, LANES), 1)).astype(BF16)

    g = gr[0] + gbr_ref[...]
    lf_hi, lf_mid, lf_lo = _split3(_log_sigmoid(g))
    prefix = _dot(lf_hi, tri_u) + _dot(lf_mid, tri_u) + _dot(lf_lo, tri_u)
    suffix = _dot(lf_hi, tri_l) + _dot(lf_mid, tri_l) + _dot(lf_lo, tri_l)
    cum = jnp.where(fwd_rows, prefix, suffix)
    b_rows = pltpu.roll(cum, N_GATE_COLS - N_HEADS, 0)
    a_all = g - b_rows
    cm_f = a_all
    cm_b = a_all
    sft = 1
    while sft < ch:
        cm_f = jnp.maximum(cm_f, jnp.where(gate_lane >= sft, pltpu.roll(cm_f, sft, 1), -jnp.inf))
        cm_b = jnp.maximum(cm_b, jnp.where(gate_lane < ch - sft, pltpu.roll(cm_b, ch - sft, 1), -jnp.inf))
        sft *= 2
    big_m = jnp.maximum(jnp.where(fwd_rows, cm_f, cm_b), 0.0)
    m_true = b_rows + big_m
    cols = _exact_transpose(jnp.concatenate([a_all, big_m, m_true], axis=0), eye_t)
    edges = (ch - 1, 0)
    m_last = [big_m[half * d:half * d + N_HEADS, edges[d]:edges[d] + 1] for d in range(2)]
    for d in range(2):
        m_new = cum[half * d + N_HEADS:half * (d + 1), edges[d]:edges[d] + 1] + m_last[d]
        mn_ref[d * N_HEADS:(d + 1) * N_HEADS, :] = jnp.broadcast_to(m_new, (N_HEADS, LANES))

    _conv_module(p, cw_ref, cb_ref, clw_ref, clb_ref, hm_ref, upad, shf, seq_len, emit_filler)

    for h in range(N_HEADS):
        emit_filler()
        hc = slice(h * hd, (h + 1) * hd)

        q = p[:, OFF_MQ + h * hd:OFF_MQ + (h + 1) * hd]
        k = p[:, OFF_MK + h * hd:OFF_MK + (h + 1) * hd]
        v = p[:, OFF_MV + h * hd:OFF_MV + (h + 1) * hd]
        v_aug = jnp.concatenate([v, ones_col], axis=1)
        qk = _dot_nt(q, k)
        kf = k.astype(F32)
        h_sum = None
        for d in range(2):
            gi = half * d + h
            mask = lower if d == 0 else upper
            a_r = a_all[gi:gi + 1, :] + log_k
            a_c = cols[:, gi:gi + 1]
            bm_c = cols[:, N_GATE_COLS + gi:N_GATE_COLS + gi + 1]
            mt_c = cols[:, 2 * N_GATE_COLS + gi:2 * N_GATE_COLS + gi + 1]
            s = qk * jnp.exp(jnp.where(mask, a_r - bm_c, -jnp.inf))
            nd = _dot(s.astype(BF16), v_aug)
            h_dir = nd[:, 0:hd] / jnp.maximum(jnp.abs(nd[:, hd:hd + 1]), jnp.exp(-mt_c))
            h_sum = h_dir if h_sum is None else h_sum + h_dir
            kw = kf * (jnp.exp(a_c - m_last[d][h:h + 1, :]) * k_scale)
            cn_ref[d, h] = _dot_tn(kw.astype(BF16), v)
            nn_ref[d * N_HEADS + h:d * N_HEADS + h + 1, :] = jnp.sum(kw, axis=0, keepdims=True)
        y = _head_norm(h_sum, mnw_ref[:, hc])
        o = p[:, OFF_MO + h * hd:OFF_MO + (h + 1) * hd].astype(F32)
        hm_ref[:, hc] = (y * jax.nn.sigmoid(o)).astype(BF16)

        rq = p[:, OFF_RQ + h * hd:OFF_RQ + (h + 1) * hd]
        rk = p[:, OFF_RK + h * hd:OFF_RK + (h + 1) * hd]
        rv = p[:, OFF_RV + h * hd:OFF_RV + (h + 1) * hd]
        scores = _dot_nt(rq, rk) * dsc[h]
        y = _head_norm(_dot(scores.astype(BF16), rv), rnw_ref[:, hc])
        gt = p[:, OFF_RG + h * hd:OFF_RG + (h + 1) * hd].astype(F32)
        hm_ref[:, 2 * BRANCH_W + h * hd:2 * BRANCH_W + (h + 1) * hd] = (y * (gt * jax.nn.sigmoid(gt))).astype(BF16)
        rkf = rk.astype(F32)
        for d in range(2):
            lg = _log_sigmoid(dec_ref[d * N_HEADS + h:d * N_HEADS + h + 1, 0:1])
            zeta = jnp.exp(lg * ((ch - 1.0) - pos_col)) if d == 0 else jnp.exp(lg * pos_col)
            sn_ref[d, h] = _dot_tn((rkf * (zeta * k_scale)).astype(BF16), rv)

    while fillers:
        emit_filler()


def _mix_sequence(p, gr, prm, state_in, hm_ref, state_out, scr, rope_tabs, *, seq_len, ch, layer, seq_idx,
                  fillers=()):
    fillers = list(fillers)

    def emit_filler():
        if fillers:
            fillers.pop(0)()

    gbr_ref, mnw_ref, cw_ref, cb_ref, clw_ref, clb_ref, dec_ref, rnw_ref = prm
    acc_a, acc_c, upad, shf, rqs, rks, cst, nst, mst, sst, dsc = scr
    has_state = state_in is not None
    if has_state:
        c0_ref, n0_ref, m0_ref, s0_ref = state_in
    else:
        cn_ref, nn_ref, mn_ref, sn_ref = state_out

    n_ch = seq_len // ch
    hd = HEAD_DIM
    k_scale = HEAD_DIM ** -0.5

    row_i = lax.broadcasted_iota(jnp.int32, (ch, ch), 0)
    col_i = lax.broadcasted_iota(jnp.int32, (ch, ch), 1)
    lower = col_i <= row_i
    upper = col_i >= row_i
    tri_l = lower.astype(BF16)
    tri_u = upper.astype(BF16)
    pos_col = lax.broadcasted_iota(jnp.int32, (ch, 1), 0).astype(F32)
    gate_lane = lax.broadcasted_iota(jnp.int32, (N_GATE_COLS, ch), 1)
    ones_col = (lax.broadcasted_iota(jnp.int32, (ch, hd), 1) == 0).astype(BF16)
    n_stack = 3 * N_GATE_COLS
    eye_t = (lax.broadcasted_iota(jnp.int32, (n_stack, LANES), 0)
             == lax.broadcasted_iota(jnp.int32, (n_stack, LANES), 1)).astype(BF16)

    _conv_module(p, cw_ref, cb_ref, clw_ref, clb_ref, hm_ref, upad, shf, seq_len, emit_filler)

    if rope_tabs is not None:
        cos_t, sin_t = rope_tabs
        even = (lax.broadcasted_iota(jnp.int32, (seq_len, hd), 1) & 1) == 0

        def rope(x):
            swapped = jnp.where(even, pltpu.roll(x, hd - 1, 1), pltpu.roll(x, 1, 1))
            return x * cos_t[...] + swapped * sin_t[...]
    else:
        def rope(x):
            return x

    for h in range(N_HEADS):
        hc = slice(h * hd, (h + 1) * hd)
        rqs[:, hc] = rope(p[:, OFF_RQ + h * hd:OFF_RQ + (h + 1) * hd].astype(F32)).astype(BF16)
        rk = rope(p[:, OFF_RK + h * hd:OFF_RK + (h + 1) * hd].astype(F32))
        rks[:, hc] = (rk * k_scale).astype(BF16)

    mst[...] = jnp.zeros(mst.shape, F32)

    for d in range(2):
        mask = lower if d == 0 else upper
        tri_row = tri_u if d == 0 else tri_l
        r0 = 2 * N_HEADS * d
        edge = ch - 1 if d == 0 else 0
        order = range(n_ch) if d == 0 else range(n_ch - 1, -1, -1)
        log_gamma = [_log_sigmoid(dec_ref[d * N_HEADS + h:d * N_HEADS + h + 1, 0:1]) for h in range(N_HEADS)]

        if has_state:
            n_hi, n_mid, n_lo = _split3(n0_ref[...])
            sel_row = lax.broadcasted_iota(jnp.int32, (2 * N_HEADS, LANES), 0)
            sel_lane = lax.broadcasted_iota(jnp.int32, (2 * N_HEADS, LANES), 1)
            for h in range(N_HEADS):
                sel = ((sel_row == d * N_HEADS + h) & (sel_lane == 0)).astype(BF16)
                cst[h, :, 0:hd] = c0_ref[d, h]
                cst[h, :, hd:2 * hd] = _dot_tn(n_hi, sel) + _dot_tn(n_mid, sel) + _dot_tn(n_lo, sel)
                sst[h] = s0_ref[d, h]
                mst[r0 + h:r0 + h + 1, :] = jnp.full((1, LANES), m0_ref[seq_idx, layer, d, h], F32)

        for step, c in enumerate(order):
            no_carry = (not has_state) and step == 0
            need_update = (step < n_ch - 1) or (not has_state)
            rows = slice(c * ch, (c + 1) * ch)

            g = gr[c] + gbr_ref[...]
            cum = _exact_right(_log_sigmoid(g), tri_row)
            b_rows = pltpu.roll(cum, N_GATE_COLS - N_HEADS, 0)
            a_all = g - b_rows
            cm = a_all
            sft = 1
            while sft < ch:
                if d == 0:
                    moved = jnp.where(gate_lane >= sft, pltpu.roll(cm, sft, 1), -jnp.inf)
                else:
                    moved = jnp.where(gate_lane < ch - sft, pltpu.roll(cm, ch - sft, 1), -jnp.inf)
                cm = jnp.maximum(cm, moved)
                sft *= 2
            m0_all = mst[:, 0:1]
            big_m = jnp.maximum(cm, m0_all)
            m_true = b_rows + big_m
            cols = _exact_transpose(jnp.concatenate([a_all, big_m, m_true], axis=0), eye_t)
            m_last4 = big_m[r0:r0 + N_HEADS, edge:edge + 1]
            m_new4 = cum[r0 + N_HEADS:r0 + 2 * N_HEADS, edge:edge + 1] + m_last4
            carry4 = jnp.exp(m0_all[r0:r0 + N_HEADS, :] - m_last4)

            for h in range(N_HEADS):
                emit_filler()
                hc = slice(h * hd, (h + 1) * hd)
                gi = r0 + h

                a_r = a_all[gi:gi + 1, :]
                a_c = cols[:, gi:gi + 1]
                bm_c = cols[:, N_GATE_COLS + gi:N_GATE_COLS + gi + 1]
                mt_c = cols[:, 2 * N_GATE_COLS + gi:2 * N_GATE_COLS + gi + 1]
                q = p[rows, OFF_MQ + h * hd:OFF_MQ + (h + 1) * hd]
                kf = p[rows, OFF_MK + h * hd:OFF_MK + (h + 1) * hd].astype(F32) * k_scale
                v = p[rows, OFF_MV + h * hd:OFF_MV + (h + 1) * hd]
                v_aug = jnp.concatenate([v, ones_col], axis=1)

                s = _dot_nt(q, kf.astype(BF16)) * jnp.exp(jnp.where(mask, a_r - bm_c, -jnp.inf))
                nd = _dot(s.astype(BF16), v_aug)
                if not no_carry:
                    c_aug = cst[h]
                    w_inter = jnp.exp(m0_all[gi:gi + 1, :] - bm_c)
                    nd = nd + _dot(q, c_aug.astype(BF16)) * w_inter
                h_dir = nd[:, 0:hd] / jnp.maximum(jnp.abs(nd[:, hd:hd + 1]), jnp.exp(-mt_c))

                if need_update:
                    kw = kf * jnp.exp(a_c - m_last4[h:h + 1, :])
                    c_new = _dot_tn(kw.astype(BF16), v_aug)
                    if not no_carry:
                        c_new = c_new + c_aug * carry4[h:h + 1, :]
                    cst[h] = c_new
                    if not has_state:
                        n_new = jnp.sum(kw, axis=0, keepdims=True)
                        if not no_carry:
                            n_new = n_new + nst[gi:gi + 1, :] * carry4[h:h + 1, :]
                        nst[gi:gi + 1, :] = n_new

                if d == 0:
                    acc_a[rows, hc] = h_dir
                else:
                    y = _head_norm(acc_a[rows, hc] + h_dir, mnw_ref[:, hc])
                    o = p[rows, OFF_MO + h * hd:OFF_MO + (h + 1) * hd].astype(F32)
                    hm_ref[rows, hc] = (y * jax.nn.sigmoid(o)).astype(BF16)

                lg = log_gamma[h]
                rq = rqs[rows, hc]
                rk = rks[rows, hc]
                rv = p[rows, OFF_RV + h * hd:OFF_RV + (h + 1) * hd]
                if d == 0:
                    scores = _dot_nt(rq, rk) * dsc[h]
                    y_dir = _dot(scores.astype(BF16), rv)
                    xi = jnp.exp(lg * (pos_col + 1.0))
                    zeta = jnp.exp(lg * ((ch - 1.0) - pos_col))
                else:
                    y_dir = None
                    xi = jnp.exp(lg * (ch - pos_col))
                    zeta = jnp.exp(lg * pos_col)
                if not no_carry:
                    s0 = sst[h]
                    y_int = _dot(rq, s0.astype(BF16)) * xi
                    y_dir = y_int if y_dir is None else y_dir + y_int
                if need_update:
                    s_new = _dot_tn((rk.astype(F32) * zeta).astype(BF16), rv)
                    if not no_carry:
                        s_new = s_new + s0 * jnp.exp(lg * float(ch))
                    sst[h] = s_new

                hcc = slice(2 * BRANCH_W + h * hd, 2 * BRANCH_W + (h + 1) * hd)
                if d == 0:
                    acc_c[rows, hc] = y_dir
                else:
                    y_sum = acc_c[rows, hc] if y_dir is None else acc_c[rows, hc] + y_dir
                    y = _head_norm(y_sum, rnw_ref[:, hc])
                    gt = p[rows, OFF_RG + h * hd:OFF_RG + (h + 1) * hd].astype(F32)
                    hm_ref[rows, hcc] = (y * (gt * jax.nn.sigmoid(gt))).astype(BF16)

            if need_update:
                mst[r0:r0 + N_HEADS, :] = jnp.broadcast_to(m_new4, (N_HEADS, LANES))

        if not has_state:
            for h in range(N_HEADS):
                cn_ref[d, h] = cst[h, :, 0:hd]
                sn_ref[d, h] = sst[h]
            nn_ref[d * N_HEADS:(d + 1) * N_HEADS, :] = nst[r0:r0 + N_HEADS, :]
            mn_ref[d * N_HEADS:(d + 1) * N_HEADS, :] = mst[r0:r0 + N_HEADS, :]

    while fillers:
        emit_filler()


def _mix_scratch(seq_len, ch):
    return [
        pltpu.VMEM((seq_len, BRANCH_W), F32),
        pltpu.VMEM((seq_len, BRANCH_W), F32),
        pltpu.VMEM((seq_len + 32, BRANCH_W), F32),
        pltpu.VMEM((SUBLANES, CONV_WIN, BRANCH_W), F32),
        pltpu.VMEM((seq_len, BRANCH_W), BF16),
        pltpu.VMEM((seq_len, BRANCH_W), BF16),
        pltpu.VMEM((N_HEADS, HEAD_DIM, 2 * HEAD_DIM), F32),
        pltpu.VMEM((N_GATE_COLS, HEAD_DIM), F32),
        pltpu.VMEM((N_GATE_COLS, LANES), F32),
        pltpu.VMEM((N_HEADS, HEAD_DIM, HEAD_DIM), F32),
        pltpu.VMEM((N_HEADS, ch, ch), F32),
    ]


N_MIX_PARAMS = 8
N_MIX_SCRATCH = 11


def _mix_param_specs(ch, index_map):
    def full(shape):
        return pl.BlockSpec(shape, lambda *a: (0,) * len(shape))

    del index_map
    return [full((N_GATE_COLS, 1)), full((1, BRANCH_W)), full((32, BRANCH_W)), full((1, BRANCH_W)),
            full((1, BRANCH_W)), full((1, BRANCH_W)), full((8, ch)), full((1, BRANCH_W))]


def _mix_param_args(params):
    return [params["gate_b_row"], params["mlstm_norm_w"], params["conv_w"], params["conv_b"],
            params["conv_ln_w"], params["conv_ln_b"], params["ret_decay"], params["ret_norm_w"]]


def _seqmix_latent_kernel(*refs, seq_len, ch, layer):
    p_ref, gr_ref = refs[:2]
    prm = refs[2:2 + N_MIX_PARAMS]
    pos = 2 + N_MIX_PARAMS
    state_in = refs[pos:pos + 4]
    hm_ref = refs[pos + 4]
    scr = refs[pos + 5:pos + 5 + N_MIX_SCRATCH]
    rope_tabs = refs[pos + 5 + N_MIX_SCRATCH:]
    b_idx = pl.program_id(0)

    @pl.when(b_idx == 0)
    def _():
        _init_tables(prm[6], scr[10], rope_tabs, seq_len=seq_len, ch=ch)

    _mix_sequence(p_ref, gr_ref, prm, state_in, hm_ref, None, scr, rope_tabs,
                  seq_len=seq_len, ch=ch, layer=layer, seq_idx=b_idx)


def _seq_mix_latent(p, gr, seq_len, layer, params, state):
    rows = p.shape[0]
    bsz = rows // seq_len
    ch = SCAN_CHUNK
    n_ch = seq_len // ch
    c0, n0, m0, s0 = state
    mat = pl.BlockSpec((None, None, 2, N_HEADS, HEAD_DIM, HEAD_DIM), lambda b: (b, layer, 0, 0, 0, 0))
    in_specs = (
        [pl.BlockSpec((seq_len, N_MIX), lambda b: (b, 0)),
         pl.BlockSpec((n_ch, N_GATE_COLS, ch), lambda b: (b, 0, 0))]
        + _mix_param_specs(ch, None)
        + [mat, pl.BlockSpec((None, None, 2 * N_HEADS, HEAD_DIM), lambda b: (b, layer, 0, 0)),
           pl.BlockSpec(memory_space=pltpu.SMEM), mat])
    args = [p, gr] + _mix_param_args(params) + [
        c0, n0.reshape(n0.shape[0], n0.shape[1], 2 * N_HEADS, HEAD_DIM), m0, s0]
    scratch = _mix_scratch(seq_len, ch) + [pltpu.VMEM((seq_len, HEAD_DIM), F32),
                                           pltpu.VMEM((seq_len, HEAD_DIM), F32)]
    return pl.pallas_call(
        functools.partial(_seqmix_latent_kernel, seq_len=seq_len, ch=ch, layer=layer),
        grid=(bsz,),
        in_specs=in_specs,
        out_specs=pl.BlockSpec((seq_len, 3 * BRANCH_W), lambda b: (b, 0)),
        out_shape=jax.ShapeDtypeStruct((rows, 3 * BRANCH_W), BF16),
        scratch_shapes=scratch,
        compiler_params=pltpu.CompilerParams(dimension_semantics=("arbitrary",)),
        name="seq_mix_latent",
    )(*args)


def _mixer_context_kernel(*refs, seq_len, ch, layer, n_carried):
    x_ref, mod_ref, w_ref, wgt_ref = refs[:4]
    prm = refs[4:4 + N_MIX_PARAMS]
    pos = 4 + N_MIX_PARAMS + n_carried
    hm_ref = refs[pos]
    state_out = refs[pos + 1:pos + 5]
    p_cur, p_next, gr_cur, gr_next, h_scr, upad, shf, dsc = refs[pos + 5:pos + 13]
    s_idx = pl.program_id(0)

    def hand_over():
        for jc in range(N_MIX // PROJ_TILE):
            cs = slice(jc * PROJ_TILE, (jc + 1) * PROJ_TILE)
            p_cur[:, cs] = p_next[:, cs]
        gr_cur[...] = gr_next[...]

    @pl.when(s_idx == 0)
    def _():
        _init_tables(prm[6], dsc, None, seq_len=seq_len, ch=ch, scale=HEAD_DIM ** -0.5)
        _project(x_ref, mod_ref, w_ref, wgt_ref, p_next, gr_next, h_scr, ch=ch)
        hand_over()

    @pl.when(s_idx > 0)
    def _():
        pieces = _project_pieces(x_ref, mod_ref, w_ref, wgt_ref, p_next, gr_next, h_scr, ch=ch)
        pieces[0]()
        _mix_context_sequence(p_cur, gr_cur, prm, hm_ref, state_out, upad, shf, dsc,
                              seq_len=seq_len, fillers=pieces[1:])
        hand_over()


def _mixer_context(x2d, mod, seq_len, layer, depth, wp, wgt, params, carried):
    rows = x2d.shape[0]
    bsz = rows // seq_len
    assert seq_len == SCAN_CHUNK, "context sequences are mixed as a single scan chunk"
    ch = seq_len
    n_ch = 1

    def nxt(s):
        return jnp.minimum(s, bsz - 1)

    def cur(s):
        return jnp.maximum(s - 1, 0)

    mat = pl.BlockSpec((None, None, 2, N_HEADS, HEAD_DIM, HEAD_DIM), lambda s: (cur(s), layer, 0, 0, 0, 0))
    vec = pl.BlockSpec((None, None, 2 * N_HEADS, HEAD_DIM), lambda s: (cur(s), layer, 0, 0))
    carried = [] if carried is None else list(carried)
    n_fixed = 4 + N_MIX_PARAMS
    in_specs = (
        [pl.BlockSpec((seq_len, D_MODEL), lambda s: (nxt(s), 0)),
         pl.BlockSpec((None, 6, D_MODEL), lambda s: (0, 0, 0)),
         _weight_spec((D_MODEL, N_MIX), layer),
         _weight_spec((N_GATE_COLS, D_MODEL), layer)]
        + _mix_param_specs(ch, None)
        + [pl.BlockSpec(memory_space=pl.ANY)] * len(carried))
    out_specs = [pl.BlockSpec((seq_len, 3 * BRANCH_W), lambda s: (cur(s), 0)), mat, vec, vec, mat]
    out_shape = [
        jax.ShapeDtypeStruct((rows, 3 * BRANCH_W), BF16),
        jax.ShapeDtypeStruct((bsz, depth, 2, N_HEADS, HEAD_DIM, HEAD_DIM), F32),
        jax.ShapeDtypeStruct((bsz, depth, 2 * N_HEADS, HEAD_DIM), F32),
        jax.ShapeDtypeStruct((bsz, depth, 2 * N_HEADS, LANES), F32),
        jax.ShapeDtypeStruct((bsz, depth, 2, N_HEADS, HEAD_DIM, HEAD_DIM), F32),
    ]
    scratch = [
        pltpu.VMEM((seq_len, N_MIX), BF16),
        pltpu.VMEM((seq_len, N_MIX), BF16),
        pltpu.VMEM((n_ch, N_GATE_COLS, ch), F32),
        pltpu.VMEM((n_ch, N_GATE_COLS, ch), F32),
        pltpu.VMEM((seq_len, D_MODEL), BF16),
        pltpu.VMEM((seq_len + 32, BRANCH_W), F32),
        pltpu.VMEM((SUBLANES, CONV_WIN, BRANCH_W), F32),
        pltpu.VMEM((N_HEADS, ch, ch), F32),
    ]
    return pl.pallas_call(
        functools.partial(_mixer_context_kernel, seq_len=seq_len, ch=ch, layer=layer, n_carried=len(carried)),
        grid=(bsz + 1,),
        in_specs=in_specs,
        out_specs=out_specs,
        out_shape=out_shape,
        scratch_shapes=scratch,
        input_output_aliases={n_fixed + i: 1 + i for i in range(len(carried))},
        compiler_params=pltpu.CompilerParams(dimension_semantics=("arbitrary",)),
        name="mixer_context",
    )(x2d, mod, wp, wgt, *_mix_param_args(params), *carried)


def _post_kernel(hm_ref, x_ref, mod_ref, wgm_ref, wa_ref, wb_ref, wc_ref, wo_ref,
                 l1w_ref, l1b_ref, w13_ref, w2_ref, l2w_ref, l2b_ref, o_ref, *, alpha):
    shift1 = mod_ref[0:1, :]
    scale1 = mod_ref[1:2, :]
    gate1 = mod_ref[2:3, :]
    shift2 = mod_ref[3:4, :]
    scale2 = mod_ref[4:5, :]
    gate2 = mod_ref[5:6, :]
    x = x_ref[...]
    h1 = (x * (1.0 + scale1) + shift1).astype(BF16)
    merged = None
    for g, w_ref in enumerate((wa_ref, wb_ref, wc_ref)):
        gate = jax.nn.sigmoid(_dot(h1, wgm_ref[:, g * D_MODEL:(g + 1) * D_MODEL]))
        term = gate * _dot(hm_ref[:, g * BRANCH_W:(g + 1) * BRANCH_W], w_ref[...])
        merged = term if merged is None else merged + term
    mix = _dot(merged.astype(BF16), wo_ref[...])
    x1 = _layer_norm(alpha * x + gate1 * mix, l1w_ref[...], l1b_ref[...])
    h2 = (x1 * (1.0 + scale2) + shift2).astype(BF16)
    ff = jnp.zeros(x1.shape, F32)
    for c in range(FFN_HIDDEN // FFN_TILE):
        a = _dot(h2, w13_ref[:, c * FFN_TILE:(c + 1) * FFN_TILE])
        gt = _dot(h2, w13_ref[:, FFN_HIDDEN + c * FFN_TILE:FFN_HIDDEN + (c + 1) * FFN_TILE])
        act = (gt * jax.nn.sigmoid(gt)) * a
        ff = ff + _dot(act.astype(BF16), w2_ref[c * FFN_TILE:(c + 1) * FFN_TILE, :])
    o_ref[...] = _layer_norm(alpha * x1 + gate2 * ff, l2w_ref[...], l2b_ref[...])


def _post(hm, x2d, mod, seq_len, layer, w, alpha):
    rows = x2d.shape[0]
    tm = ROW_TILE
    n_seq = mod.shape[0]

    def seq_of(i):
        return (i * tm) // seq_len if n_seq > 1 else 0

    vec = _weight_spec((1, D_MODEL), layer)
    return pl.pallas_call(
        functools.partial(_post_kernel, alpha=alpha),
        grid=(rows // tm,),
        in_specs=[
            pl.BlockSpec((tm, 3 * BRANCH_W), lambda i: (i, 0)),
            pl.BlockSpec((tm, D_MODEL), lambda i: (i, 0)),
            pl.BlockSpec((None, 6, D_MODEL), lambda i: (seq_of(i), 0, 0)),
            _weight_spec((D_MODEL, N_GM), layer),
            _weight_spec((BRANCH_W, D_MODEL), layer), _weight_spec((BRANCH_W, D_MODEL), layer),
            _weight_spec((BRANCH_W, D_MODEL), layer),
            _weight_spec((D_MODEL, D_MODEL), layer), vec, vec,
            _weight_spec((D_MODEL, 2 * FFN_HIDDEN), layer), _weight_spec((FFN_HIDDEN, D_MODEL), layer), vec, vec,
        ],
        out_specs=pl.BlockSpec((tm, D_MODEL), lambda i: (i, 0)),
        out_shape=jax.ShapeDtypeStruct((rows, D_MODEL), F32),
        compiler_params=pltpu.CompilerParams(dimension_semantics=("parallel",)),
        name="post",
    )(hm, x2d, mod, w["gm_w"], w["mlstm_out_w"], w["conv_out_w"], w["ret_out_w"], w["out_w"],
      w["ln1_w"], w["ln1_b"], w["ffn_w13"], w["ffn_w2"], w["ln2_w"], w["ln2_b"])


def kernel(x_prompt, x_sample, state_mlstm_C, state_mlstm_n, state_mlstm_m, state_ret_S, c, c_ctx, ada_w, ada_b, in_w, mlstm_gate_b, mlstm_norm_w, mlstm_out_w, conv_w, conv_b, conv_ln_w, conv_ln_b, conv_out_w, ret_decay, ret_norm_w, ret_out_w, out_w, ln1_w, ln1_b, ln2_w, ln2_b, ffn_w13, ffn_w2):
    depth = in_w.shape[0]
    alpha = (2.0 * depth) ** 0.25
    bsz, seq, _ = x_prompt.shape
    dbsz, dseq, _ = x_sample.shape
    ch = SCAN_CHUNK
    wp, gm_w, wgt = _split_in_w(in_w)
    dense = {
        "gm_w": gm_w,
        "mlstm_out_w": mlstm_out_w.astype(BF16), "conv_out_w": conv_out_w.astype(BF16),
        "ret_out_w": ret_out_w.astype(BF16), "out_w": out_w.astype(BF16),
        "ffn_w13": ffn_w13.astype(BF16), "ffn_w2": ffn_w2.astype(BF16),
        "ln1_w": ln1_w.reshape(depth, 1, D_MODEL), "ln1_b": ln1_b.reshape(depth, 1, D_MODEL),
        "ln2_w": ln2_w.reshape(depth, 1, D_MODEL), "ln2_b": ln2_b.reshape(depth, 1, D_MODEL),
    }
    gate_b_flat = mlstm_gate_b.reshape(depth, N_GATE_COLS)

    c_all = jnp.concatenate([c_ctx[None, :], c, jnp.zeros((8 - 1 - dbsz, D_MODEL), F32)], axis=0)
    mod = _ada_mod(c_all, ada_w, ada_b)

    y_prompt = x_prompt.reshape(bsz * seq, D_MODEL)
    y_sample = x_sample.reshape(dbsz * dseq, D_MODEL)
    states = None
    for l in range(depth):
        mix_params = {
            "gate_b_row": gate_b_flat[l][:, None],
            "mlstm_norm_w": mlstm_norm_w[l][None, :],
            "conv_w": jnp.pad(conv_w[l], ((0, 32 - CONV_K), (0, 0))),
            "conv_b": conv_b[l][None, :],
            "conv_ln_w": conv_ln_w[l][None, :],
            "conv_ln_b": conv_ln_b[l][None, :],
            "ret_decay": jnp.broadcast_to(ret_decay[l].reshape(2 * N_HEADS, 1), (2 * N_HEADS, ch)),
            "ret_norm_w": ret_norm_w[l][None, :],
        }
        mod_ctx = mod[l, 0:1].reshape(1, 6, D_MODEL)
        mod_lat = mod[l, 1:1 + dbsz].reshape(dbsz, 6, D_MODEL)

        hm, *states = _mixer_context(y_prompt, mod_ctx, seq, l, depth, wp, wgt, mix_params, states)
        y_prompt = _post(hm, y_prompt, mod_ctx, seq, l, dense, alpha)

        p, gr = _in_proj(y_sample, mod_lat, dseq, l, wp, wgt)
        hm = _seq_mix_latent(p, gr, dseq, l, mix_params,
                             (state_mlstm_C, state_mlstm_n, state_mlstm_m, state_ret_S))
        y_sample = _post(hm, y_sample, mod_lat, dseq, l, dense, alpha)

    new_c, new_n, new_m, new_s = states
    return (y_prompt.reshape(bsz, seq, D_MODEL), y_sample.reshape(dbsz, dseq, D_MODEL),
            new_c, new_n.reshape(bsz, depth, 2, N_HEADS, HEAD_DIM),
            new_m[:, :, :, 0].reshape(bsz, depth, 2, N_HEADS), new_s)
```

```python
import functools
import math

import jax
import jax.numpy as jnp
from jax import lax
from jax.experimental import pallas as pl
from jax.experimental.pallas import tpu as pltpu

F32 = jnp.float32
BF16 = jnp.bfloat16

D_MODEL = 1024
N_HEADS = 4
HEAD_DIM = 128
BRANCH_W = N_HEADS * HEAD_DIM
N_GATE_COLS = 4 * N_HEADS
GATE_LO = 4 * BRANCH_W
CONV_K = 31
CONV_PAD = CONV_K // 2
FFN_HIDDEN = 2816
GRID_W = 64
ROPE_BASE = 10000.0
LN_EPS = 1e-5
N_MIX = 10 * BRANCH_W
N_GM = 3 * D_MODEL
LANES = 128
SUBLANES = 8

OFF_MQ, OFF_MK, OFF_MV, OFF_MO = 0, 512, 1024, 1536
OFF_CA, OFF_CG = 2048, 2560
OFF_RQ, OFF_RK, OFF_RV, OFF_RG = 3072, 3584, 4096, 4608

ROW_TILE = 512
PROJ_TILE = 1024
SCAN_CHUNK = 256
CONV_ROWS = 128
CONV_WIN = CONV_ROWS + 24
FFN_TILE = 256


def _dot(a, b):
    return jnp.dot(a, b, preferred_element_type=F32)


def _dot_nt(a, b):
    return lax.dot_general(a, b, (((1,), (1,)), ((), ())), preferred_element_type=F32)


def _dot_tn(a, b):
    return lax.dot_general(a, b, (((0,), (0,)), ((), ())), preferred_element_type=F32)


def _layer_norm(z, w, b):
    mu = jnp.mean(z, axis=-1, keepdims=True)
    zc = z - mu
    var = jnp.mean(zc * zc, axis=-1, keepdims=True)
    return zc * lax.rsqrt(var + LN_EPS) * w + b


def _head_norm(y, w):
    mu = jnp.mean(y, axis=-1, keepdims=True)
    yc = y - mu
    var = jnp.mean(yc * yc, axis=-1, keepdims=True)
    return yc * lax.rsqrt(var + LN_EPS) * w


def _log_sigmoid(x):
    return jnp.minimum(x, 0.0) - jnp.log1p(jnp.exp(-jnp.abs(x)))


def _split3(x):
    hi = x.astype(BF16)
    r = x - hi.astype(F32)
    mid = r.astype(BF16)
    lo = (r - mid.astype(F32)).astype(BF16)
    return hi, mid, lo


def _exact_right(x, sel):
    hi, mid, lo = _split3(x)
    return _dot(hi, sel) + _dot(mid, sel) + _dot(lo, sel)


def _exact_transpose(x, sel):
    hi, mid, lo = _split3(x)
    return _dot_tn(hi, sel) + _dot_tn(mid, sel) + _dot_tn(lo, sel)


def _ada_kernel(c_ref, w_ref, b_ref, o_ref):
    c = c_ref[...]
    s = c * jax.nn.sigmoid(c)
    o_ref[...] = _dot(s.astype(BF16), w_ref[...].astype(BF16)) + b_ref[...]


def _ada_mod(c_all, ada_w, ada_b):
    depth, _, n = ada_w.shape
    tn = 1024
    return pl.pallas_call(
        _ada_kernel,
        grid=(depth, n // tn),
        in_specs=[
            pl.BlockSpec((8, D_MODEL), lambda l, j: (0, 0)),
            pl.BlockSpec((None, D_MODEL, tn), lambda l, j: (l, 0, j)),
            pl.BlockSpec((None, 1, tn), lambda l, j: (l, 0, j)),
        ],
        out_specs=pl.BlockSpec((None, 8, tn), lambda l, j: (l, 0, j)),
        out_shape=jax.ShapeDtypeStruct((depth, 8, n), F32),
        name="ada_mod",
    )(c_all, ada_w, ada_b.reshape(depth, 1, n))


def _split_in_w_kernel(wt_ref, wp_ref, gm_ref, wgt_ref):
    gate_hi = GATE_LO + N_GATE_COLS
    gm_lo = gate_hi + N_MIX - GATE_LO
    wp_ref[:, 0:GATE_LO] = wt_ref[0:GATE_LO, :].T.astype(BF16)
    wp_ref[:, GATE_LO:N_MIX] = wt_ref[gate_hi:gm_lo, :].T.astype(BF16)
    gm_ref[...] = wt_ref[gm_lo:gm_lo + N_GM, :].T.astype(BF16)
    wgt_ref[...] = wt_ref[GATE_LO:gate_hi, :].astype(BF16)


def _split_in_w(in_w):
    depth, d_in, n_in = in_w.shape
    tr = 128
    in_wt = jnp.swapaxes(in_w, 1, 2)
    return pl.pallas_call(
        _split_in_w_kernel,
        grid=(depth, d_in // tr),
        in_specs=[pl.BlockSpec((None, n_in, tr), lambda l, i: (l, 0, i))],
        out_specs=[pl.BlockSpec((None, tr, N_MIX), lambda l, i: (l, i, 0)),
                   pl.BlockSpec((None, tr, N_GM), lambda l, i: (l, i, 0)),
                   pl.BlockSpec((None, N_GATE_COLS, tr), lambda l, i: (l, 0, i))],
        out_shape=[jax.ShapeDtypeStruct((depth, d_in, N_MIX), BF16),
                   jax.ShapeDtypeStruct((depth, d_in, N_GM), BF16),
                   jax.ShapeDtypeStruct((depth, N_GATE_COLS, d_in), BF16)],
        name="split_in_w",
    )(in_wt)


def _project_pieces(x_ref, mod_ref, w_ref, wgt_ref, p_out, gr_out, h_scr, *, ch):
    n_rows = x_ref.shape[0]

    def modulate():
        shift = mod_ref[0:1, :]
        scale = mod_ref[1:2, :]
        h_scr[...] = (x_ref[...] * (1.0 + scale) + shift).astype(BF16)
        for cc in range(n_rows // ch):
            gr_out[cc] = _dot_nt(wgt_ref[...], h_scr[cc * ch:(cc + 1) * ch, :])

    def column_tile(jc):
        cs = slice(jc * PROJ_TILE, (jc + 1) * PROJ_TILE)
        p_out[:, cs] = _dot(h_scr[...], w_ref[:, cs]).astype(BF16)

    return [modulate] + [functools.partial(column_tile, jc) for jc in range(N_MIX // PROJ_TILE)]


def _project(x_ref, mod_ref, w_ref, wgt_ref, p_out, gr_out, h_scr, *, ch):
    for piece in _project_pieces(x_ref, mod_ref, w_ref, wgt_ref, p_out, gr_out, h_scr, ch=ch):
        piece()


def _inproj_kernel(x_ref, mod_ref, w_ref, wgt_ref, p_ref, gr_ref, h_scr, *, ch):
    _project(x_ref, mod_ref, w_ref, wgt_ref, p_ref, gr_ref, h_scr, ch=ch)


def _weight_spec(shape, layer):
    nd = len(shape)
    return pl.BlockSpec((None,) + shape, lambda *_: (layer,) + (0,) * nd, pipeline_mode=pl.Buffered(1))


def _in_proj(x2d, mod, seq_len, layer, wp, wgt):
    rows = x2d.shape[0]
    tm, ch = ROW_TILE, SCAN_CHUNK
    n_seq = mod.shape[0]

    def seq_of(i):
        return (i * tm) // seq_len if n_seq > 1 else 0

    return pl.pallas_call(
        functools.partial(_inproj_kernel, ch=ch),
        grid=(rows // tm,),
        in_specs=[
            pl.BlockSpec((tm, D_MODEL), lambda i: (i, 0)),
            pl.BlockSpec((None, 6, D_MODEL), lambda i: (seq_of(i), 0, 0)),
            _weight_spec((D_MODEL, N_MIX), layer),
            _weight_spec((N_GATE_COLS, D_MODEL), layer),
        ],
        out_specs=[
            pl.BlockSpec((tm, N_MIX), lambda i: (i, 0)),
            pl.BlockSpec((tm // ch, N_GATE_COLS, ch), lambda i: (i, 0, 0)),
        ],
        out_shape=[
            jax.ShapeDtypeStruct((rows, N_MIX), BF16),
            jax.ShapeDtypeStruct((rows // ch, N_GATE_COLS, ch), F32),
        ],
        scratch_shapes=[pltpu.VMEM((tm, D_MODEL), BF16)],
        compiler_params=pltpu.CompilerParams(dimension_semantics=("parallel",)),
        name="in_proj",
    )(x2d, mod, wp, wgt)


def _init_tables(dec_ref, dsc, rope_tabs, *, seq_len, ch, scale=None):
    hd = HEAD_DIM
    row_i = lax.broadcasted_iota(jnp.int32, (ch, ch), 0)
    col_i = lax.broadcasted_iota(jnp.int32, (ch, ch), 1)
    diff = (row_i - col_i).astype(F32)
    for h in range(N_HEADS):
        lg_f = _log_sigmoid(dec_ref[h:h + 1, :])
        lg_b = _log_sigmoid(dec_ref[N_HEADS + h:N_HEADS + h + 1, :])
        decay = (jnp.where(col_i <= row_i, jnp.exp(lg_f * jnp.maximum(diff, 0.0)), 0.0)
                 + jnp.where(col_i >= row_i, jnp.exp(lg_b * jnp.maximum(-diff, 0.0)), 0.0))
        dsc[h] = decay if scale is None else decay * scale
    if rope_tabs is not None:
        cos_t, sin_t = rope_tabs
        t_i = lax.broadcasted_iota(jnp.int32, (seq_len, hd), 0)
        lane = lax.broadcasted_iota(jnp.int32, (seq_len, hd), 1)
        pair = lane >> 1
        n_pairs = hd // 4
        freq = jnp.exp((pair & (n_pairs - 1)).astype(F32) * (-math.log(ROPE_BASE) / n_pairs))
        grid_pos = jnp.where(pair < n_pairs, t_i // GRID_W, t_i % GRID_W).astype(F32)
        ang = grid_pos * freq
        sin = jnp.sin(ang)
        cos_t[...] = jnp.cos(ang)
        sin_t[...] = jnp.where((lane & 1) == 0, -sin, sin)


def _conv_module(p, cw_ref, cb_ref, clw_ref, clb_ref, hm_ref, upad, shf, seq_len, emit_filler):
    zeros_pad = jnp.zeros((16, BRANCH_W), F32)
    upad[0:16, :] = zeros_pad
    upad[seq_len + 16:seq_len + 32, :] = zeros_pad
    for rb in range(seq_len // CONV_ROWS):
        r = slice(rb * CONV_ROWS, (rb + 1) * CONV_ROWS)
        ca = p[r, OFF_CA:OFF_CA + BRANCH_W].astype(F32)
        cg = p[r, OFF_CG:OFF_CG + BRANCH_W].astype(F32)
        upad[16 + rb * CONV_ROWS:16 + (rb + 1) * CONV_ROWS, :] = ca * jax.nn.sigmoid(cg)
    for rb in range(seq_len // CONV_ROWS):
        emit_filler()
        base = rb * CONV_ROWS
        for sft in range(1, SUBLANES):
            shf[sft] = upad[base + sft:base + sft + CONV_WIN, :]
        acc = None
        for k in range(CONV_K):
            a8, sft = divmod(k + 16 - CONV_PAD, SUBLANES)
            if sft == 0:
                win = upad[base + a8 * SUBLANES:base + a8 * SUBLANES + CONV_ROWS, :]
            else:
                win = shf[sft, a8 * SUBLANES:a8 * SUBLANES + CONV_ROWS, :]
            term = win * cw_ref[k:k + 1, :]
            acc = term if acc is None else acc + term
        acc = acc + cb_ref[...]
        u = _layer_norm(acc, clw_ref[...], clb_ref[...])
        u = u * jax.nn.sigmoid(u)
        hm_ref[rb * CONV_ROWS:(rb + 1) * CONV_ROWS, BRANCH_W:2 * BRANCH_W] = u.astype(BF16)


def _mix_context_sequence(p, gr, prm, hm_ref, state_out, upad, shf, dsc, *, seq_len, fillers=()):
    fillers = list(fillers)

    def emit_filler():
        if fillers:
            fillers.pop(0)()

    gbr_ref, mnw_ref, cw_ref, cb_ref, clw_ref, clb_ref, dec_ref, rnw_ref = prm
    cn_ref, nn_ref, mn_ref, sn_ref = state_out
    ch = seq_len
    hd = HEAD_DIM
    k_scale = HEAD_DIM ** -0.5
    log_k = math.log(k_scale)
    half = 2 * N_HEADS

    row_i = lax.broadcasted_iota(jnp.int32, (ch, ch), 0)
    col_i = lax.broadcasted_iota(jnp.int32, (ch, ch), 1)
    lower = col_i <= row_i
    upper = col_i >= row_i
    tri_l = lower.astype(BF16)
    tri_u = upper.astype(BF16)
    pos_col = lax.broadcasted_iota(jnp.int32, (ch, 1), 0).astype(F32)
    gate_lane = lax.broadcasted_iota(jnp.int32, (N_GATE_COLS, ch), 1)
    fwd_rows = lax.broadcasted_iota(jnp.int32, (N_GATE_COLS, ch), 0) < half
    ones_col = (lax.broadcasted_iota(jnp.int32, (ch, hd), 1) == 0).astype(BF16)
    n_stack = 3 * N_GATE_COLS
    eye_t = (lax.broadcasted_iota(jnp.int32, (n_stack, LANES), 0)
             == lax.broadcasted_iota(jnp.int32, (n_stack, LANES), 1)).astype(BF16)

    g = gr[0] + gbr_ref[...]
    lf_hi, lf_mid, lf_lo = _split3(_log_sigmoid(g))
    prefix = _dot(lf_hi, tri_u) + _dot(lf_mid, tri_u) + _dot(lf_lo, tri_u)
    suffix = _dot(lf_hi, tri_l) + _dot(lf_mid, tri_l) + _dot(lf_lo, tri_l)
    cum = jnp.where(fwd_rows, prefix, suffix)
    b_rows = pltpu.roll(cum, N_GATE_COLS - N_HEADS, 0)
    a_all = g - b_rows
    cm_f = a_all
    cm_b = a_all
    sft = 1
    while sft < ch:
        cm_f = jnp.maximum(cm_f, jnp.where(gate_lane >= sft, pltpu.roll(cm_f, sft, 1), -jnp.inf))
        cm_b = jnp.maximum(cm_b, jnp.where(gate_lane < ch - sft, pltpu.roll(cm_b, ch - sft, 1), -jnp.inf))
        sft *= 2
    big_m = jnp.maximum(jnp.where(fwd_rows, cm_f, cm_b), 0.0)
    m_true = b_rows + big_m
    cols = _exact_transpose(jnp.concatenate([a_all, big_m, m_true], axis=0), eye_t)
    edges = (ch - 1, 0)
    m_last = [big_m[half * d:half * d + N_HEADS, edges[d]:edges[d] + 1] for d in range(2)]
    for d in range(2):
        m_new = cum[half * d + N_HEADS:half * (d + 1), edges[d]:edges[d] + 1] + m_last[d]
        mn_ref[d * N_HEADS:(d + 1) * N_HEADS, :] = jnp.broadcast_to(m_new, (N_HEADS, LANES))

    _conv_module(p, cw_ref, cb_ref, clw_ref, clb_ref, hm_ref, upad, shf, seq_len, emit_filler)

    for h in range(N_HEADS):
        emit_filler()
        hc = slice(h * hd, (h + 1) * hd)

        q = p[:, OFF_MQ + h * hd:OFF_MQ + (h + 1) * hd]
        k = p[:, OFF_MK + h * hd:OFF_MK + (h + 1) * hd]
        v = p[:, OFF_MV + h * hd:OFF_MV + (h + 1) * hd]
        v_aug = jnp.concatenate([v, ones_col], axis=1)
        qk = _dot_nt(q, k)
        kf = k.astype(F32)
        h_sum = None
        for d in range(2):
            gi = half * d + h
            mask = lower if d == 0 else upper
            a_r = a_all[gi:gi + 1, :] + log_k
            a_c = cols[:, gi:gi + 1]
            bm_c = cols[:, N_GATE_COLS + gi:N_GATE_COLS + gi + 1]
            mt_c = cols[:, 2 * N_GATE_COLS + gi:2 * N_GATE_COLS + gi + 1]
            s = qk * jnp.exp(jnp.where(mask, a_r - bm_c, -jnp.inf))
            nd = _dot(s.astype(BF16), v_aug)
            h_dir = nd[:, 0:hd] / jnp.maximum(jnp.abs(nd[:, hd:hd + 1]), jnp.exp(-mt_c))
            h_sum = h_dir if h_sum is None else h_sum + h_dir
            kw = kf * (jnp.exp(a_c - m_last[d][h:h + 1, :]) * k_scale)
            cn_ref[d, h] = _dot_tn(kw.astype(BF16), v)
            nn_ref[d * N_HEADS + h:d * N_HEADS + h + 1, :] = jnp.sum(kw, axis=0, keepdims=True)
        y = _head_norm(h_sum, mnw_ref[:, hc])
        o = p[:, OFF_MO + h * hd:OFF_MO + (h + 1) * hd].astype(F32)
        hm_ref[:, hc] = (y * jax.nn.sigmoid(o)).astype(BF16)

        rq = p[:, OFF_RQ + h * hd:OFF_RQ + (h + 1) * hd]
        rk = p[:, OFF_RK + h * hd:OFF_RK + (h + 1) * hd]
        rv = p[:, OFF_RV + h * hd:OFF_RV + (h + 1) * hd]
        scores = _dot_nt(rq, rk) * dsc[h]
        y = _head_norm(_dot(scores.astype(BF16), rv), rnw_ref[:, hc])
        gt = p[:, OFF_RG + h * hd:OFF_RG + (h + 1) * hd].astype(F32)
        hm_ref[:, 2 * BRANCH_W + h * hd:2 * BRANCH_W + (h + 1) * hd] = (y * (gt * jax.nn.sigmoid(gt))).astype(BF16)
        rkf = rk.astype(F32)
        for d in range(2):
            lg = _log_sigmoid(dec_ref[d * N_HEADS + h:d * N_HEADS + h + 1, 0:1])
            zeta = jnp.exp(lg * ((ch - 1.0) - pos_col)) if d == 0 else jnp.exp(lg * pos_col)
            sn_ref[d, h] = _dot_tn((rkf * (zeta * k_scale)).astype(BF16), rv)

    while fillers:
        emit_filler()


def _mix_sequence(p, gr, prm, state_in, hm_ref, state_out, scr, rope_tabs, *, seq_len, ch, layer, seq_idx,
                  fillers=()):
    fillers = list(fillers)

    def emit_filler():
        if fillers:
            fillers.pop(0)()

    gbr_ref, mnw_ref, cw_ref, cb_ref, clw_ref, clb_ref, dec_ref, rnw_ref = prm
    acc_a, acc_c, upad, shf, rqs, rks, cst, nst, mst, sst, dsc = scr
    has_state = state_in is not None
    if has_state:
        c0_ref, n0_ref, m0_ref, s0_ref = state_in
    else:
        cn_ref, nn_ref, mn_ref, sn_ref = state_out

    n_ch = seq_len // ch
    hd = HEAD_DIM
    k_scale = HEAD_DIM ** -0.5

    row_i = lax.broadcasted_iota(jnp.int32, (ch, ch), 0)
    col_i = lax.broadcasted_iota(jnp.int32, (ch, ch), 1)
    lower = col_i <= row_i
    upper = col_i >= row_i
    tri_l = lower.astype(BF16)
    tri_u = upper.astype(BF16)
    pos_col = lax.broadcasted_iota(jnp.int32, (ch, 1), 0).astype(F32)
    gate_lane = lax.broadcasted_iota(jnp.int32, (N_GATE_COLS, ch), 1)
    ones_col = (lax.broadcasted_iota(jnp.int32, (ch, hd), 1) == 0).astype(BF16)
    n_stack = 3 * N_GATE_COLS
    eye_t = (lax.broadcasted_iota(jnp.int32, (n_stack, LANES), 0)
             == lax.broadcasted_iota(jnp.int32, (n_stack, LANES), 1)).astype(BF16)

    _conv_module(p, cw_ref, cb_ref, clw_ref, clb_ref, hm_ref, upad, shf, seq_len, emit_filler)

    if rope_tabs is not None:
        cos_t, sin_t = rope_tabs
        even = (lax.broadcasted_iota(jnp.int32, (seq_len, hd), 1) & 1) == 0

        def rope(x):
            swapped = jnp.where(even, pltpu.roll(x, hd - 1, 1), pltpu.roll(x, 1, 1))
            return x * cos_t[...] + swapped * sin_t[...]
    else:
        def rope(x):
            return x

    for h in range(N_HEADS):
        hc = slice(h * hd, (h + 1) * hd)
        rqs[:, hc] = rope(p[:, OFF_RQ + h * hd:OFF_RQ + (h + 1) * hd].astype(F32)).astype(BF16)
        rk = rope(p[:, OFF_RK + h * hd:OFF_RK + (h + 1) * hd].astype(F32))
        rks[:, hc] = (rk * k_scale).astype(BF16)

    mst[...] = jnp.zeros(mst.shape, F32)

    for d in range(2):
        mask = lower if d == 0 else upper
        tri_row = tri_u if d == 0 else tri_l
        r0 = 2 * N_HEADS * d
        edge = ch - 1 if d == 0 else 0
        order = range(n_ch) if d == 0 else range(n_ch - 1, -1, -1)
        log_gamma = [_log_sigmoid(dec_ref[d * N_HEADS + h:d * N_HEADS + h + 1, 0:1]) for h in range(N_HEADS)]

        if has_state:
            n_hi, n_mid, n_lo = _split3(n0_ref[...])
            sel_row = lax.broadcasted_iota(jnp.int32, (2 * N_HEADS, LANES), 0)
            sel_lane = lax.broadcasted_iota(jnp.int32, (2 * N_HEADS, LANES), 1)
            for h in range(N_HEADS):
                sel = ((sel_row == d * N_HEADS + h) & (sel_lane == 0)).astype(BF16)
                cst[h, :, 0:hd] = c0_ref[d, h]
                cst[h, :, hd:2 * hd] = _dot_tn(n_hi, sel) + _dot_tn(n_mid, sel) + _dot_tn(n_lo, sel)
                sst[h] = s0_ref[d, h]
                mst[r0 + h:r0 + h + 1, :] = jnp.full((1, LANES), m0_ref[seq_idx, layer, d, h], F32)

        for step, c in enumerate(order):
            no_carry = (not has_state) and step == 0
            need_update = (step < n_ch - 1) or (not has_state)
            rows = slice(c * ch, (c + 1) * ch)

            g = gr[c] + gbr_ref[...]
            cum = _exact_right(_log_sigmoid(g), tri_row)
            b_rows = pltpu.roll(cum, N_GATE_COLS - N_HEADS, 0)
            a_all = g - b_rows
            cm = a_all
            sft = 1
            while sft < ch:
                if d == 0:
                    moved = jnp.where(gate_lane >= sft, pltpu.roll(cm, sft, 1), -jnp.inf)
                else:
                    moved = jnp.where(gate_lane < ch - sft, pltpu.roll(cm, ch - sft, 1), -jnp.inf)
                cm = jnp.maximum(cm, moved)
                sft *= 2
            m0_all = mst[:, 0:1]
            big_m = jnp.maximum(cm, m0_all)
            m_true = b_rows + big_m
            cols = _exact_transpose(jnp.concatenate([a_all, big_m, m_true], axis=0), eye_t)
            m_last4 = big_m[r0:r0 + N_HEADS, edge:edge + 1]
            m_new4 = cum[r0 + N_HEADS:r0 + 2 * N_HEADS, edge:edge + 1] + m_last4
            carry4 = jnp.exp(m0_all[r0:r0 + N_HEADS, :] - m_last4)

            for h in range(N_HEADS):
                emit_filler()
                hc = slice(h * hd, (h + 1) * hd)
                gi = r0 + h

                a_r = a_all[gi:gi + 1, :]
                a_c = cols[:, gi:gi + 1]
                bm_c = cols[:, N_GATE_COLS + gi:N_GATE_COLS + gi + 1]
                mt_c = cols[:, 2 * N_GATE_COLS + gi:2 * N_GATE_COLS + gi + 1]
                q = p[rows, OFF_MQ + h * hd:OFF_MQ + (h + 1) * hd]
                kf = p[rows, OFF_MK + h * hd:OFF_MK + (h + 1) * hd].astype(F32) * k_scale
                v = p[rows, OFF_MV + h * hd:OFF_MV + (h + 1) * hd]
                v_aug = jnp.concatenate([v, ones_col], axis=1)

                s = _dot_nt(q, kf.astype(BF16)) * jnp.exp(jnp.where(mask, a_r - bm_c, -jnp.inf))
                nd = _dot(s.astype(BF16), v_aug)
                if not no_carry:
                    c_aug = cst[h]
                    w_inter = jnp.exp(m0_all[gi:gi + 1, :] - bm_c)
                    nd = nd + _dot(q, c_aug.astype(BF16)) * w_inter
                h_dir = nd[:, 0:hd] / jnp.maximum(jnp.abs(nd[:, hd:hd + 1]), jnp.exp(-mt_c))

                if need_update:
                    kw = kf * jnp.exp(a_c - m_last4[h:h + 1, :])
                    c_new = _dot_tn(kw.astype(BF16), v_aug)
                    if not no_carry:
                        c_new = c_new + c_aug * carry4[h:h + 1, :]
                    cst[h] = c_new
                    if not has_state:
                        n_new = jnp.sum(kw, axis=0, keepdims=True)
                        if not no_carry:
                            n_new = n_new + nst[gi:gi + 1, :] * carry4[h:h + 1, :]
                        nst[gi:gi + 1, :] = n_new

                if d == 0:
                    acc_a[rows, hc] = h_dir
                else:
                    y = _head_norm(acc_a[rows, hc] + h_dir, mnw_ref[:, hc])
                    o = p[rows, OFF_MO + h * hd:OFF_MO + (h + 1) * hd].astype(F32)
                    hm_ref[rows, hc] = (y * jax.nn.sigmoid(o)).astype(BF16)

                lg = log_gamma[h]
                rq = rqs[rows, hc]
                rk = rks[rows, hc]
                rv = p[rows, OFF_RV + h * hd:OFF_RV + (h + 1) * hd]
                if d == 0:
                    scores = _dot_nt(rq, rk) * dsc[h]
                    y_dir = _dot(scores.astype(BF16), rv)
                    xi = jnp.exp(lg * (pos_col + 1.0))
                    zeta = jnp.exp(lg * ((ch - 1.0) - pos_col))
                else:
                    y_dir = None
                    xi = jnp.exp(lg * (ch - pos_col))
                    zeta = jnp.exp(lg * pos_col)
                if not no_carry:
                    s0 = sst[h]
                    y_int = _dot(rq, s0.astype(BF16)) * xi
                    y_dir = y_int if y_dir is None else y_dir + y_int
                if need_update:
                    s_new = _dot_tn((rk.astype(F32) * zeta).astype(BF16), rv)
                    if not no_carry:
                        s_new = s_new + s0 * jnp.exp(lg * float(ch))
                    sst[h] = s_new

                hcc = slice(2 * BRANCH_W + h * hd, 2 * BRANCH_W + (h + 1) * hd)
                if d == 0:
                    acc_c[rows, hc] = y_dir
                else:
                    y_sum = acc_c[rows, hc] if y_dir is None else acc_c[rows, hc] + y_dir
                    y = _head_norm(y_sum, rnw_ref[:, hc])
                    gt = p[rows, OFF_RG + h * hd:OFF_RG + (h + 1) * hd].astype(F32)
                    hm_ref[rows, hcc] = (y * (gt * jax.nn.sigmoid(gt))).astype(BF16)

            if need_update:
                mst[r0:r0 + N_HEADS, :] = jnp.broadcast_to(m_new4, (N_HEADS, LANES))

        if not has_state:
            for h in range(N_HEADS):
                cn_ref[d, h] = cst[h, :, 0:hd]
                sn_ref[d, h] = sst[h]
            nn_ref[d * N_HEADS:(d + 1) * N_HEADS, :] = nst[r0:r0 + N_HEADS, :]
            mn_ref[d * N_HEADS:(d + 1) * N_HEADS, :] = mst[r0:r0 + N_HEADS, :]

    while fillers:
        emit_filler()


def _mix_scratch(seq_len, ch):
    return [
        pltpu.VMEM((seq_len, BRANCH_W), F32),
        pltpu.VMEM((seq_len, BRANCH_W), F32),
        pltpu.VMEM((seq_len + 32, BRANCH_W), F32),
        pltpu.VMEM((SUBLANES, CONV_WIN, BRANCH_W), F32),
        pltpu.VMEM((seq_len, BRANCH_W), BF16),
        pltpu.VMEM((seq_len, BRANCH_W), BF16),
        pltpu.VMEM((N_HEADS, HEAD_DIM, 2 * HEAD_DIM), F32),
        pltpu.VMEM((N_GATE_COLS, HEAD_DIM), F32),
        pltpu.VMEM((N_GATE_COLS, LANES), F32),
        pltpu.VMEM((N_HEADS, HEAD_DIM, HEAD_DIM), F32),
        pltpu.VMEM((N_HEADS, ch, ch), F32),
    ]


N_MIX_PARAMS = 8
N_MIX_SCRATCH = 11


def _mix_param_specs(ch, index_map):
    def full(shape):
        return pl.BlockSpec(shape, lambda *a: (0,) * len(shape))

    del index_map
    return [full((N_GATE_COLS, 1)), full((1, BRANCH_W)), full((32, BRANCH_W)), full((1, BRANCH_W)),
            full((1, BRANCH_W)), full((1, BRANCH_W)), full((8, ch)), full((1, BRANCH_W))]


def _mix_param_args(params):
    return [params["gate_b_row"], params["mlstm_norm_w"], params["conv_w"], params["conv_b"],
            params["conv_ln_w"], params["conv_ln_b"], params["ret_decay"], params["ret_norm_w"]]


def _seqmix_latent_kernel(*refs, seq_len, ch, layer):
    p_ref, gr_ref = refs[:2]
    prm = refs[2:2 + N_MIX_PARAMS]
    pos = 2 + N_MIX_PARAMS
    state_in = refs[pos:pos + 4]
    hm_ref = refs[pos + 4]
    scr = refs[pos + 5:pos + 5 + N_MIX_SCRATCH]
    rope_tabs = refs[pos + 5 + N_MIX_SCRATCH:]
    b_idx = pl.program_id(0)

    @pl.when(b_idx == 0)
    def _():
        _init_tables(prm[6], scr[10], rope_tabs, seq_len=seq_len, ch=ch)

    _mix_sequence(p_ref, gr_ref, prm, state_in, hm_ref, None, scr, rope_tabs,
                  seq_len=seq_len, ch=ch, layer=layer, seq_idx=b_idx)


def _seq_mix_latent(p, gr, seq_len, layer, params, state):
    rows = p.shape[0]
    bsz = rows // seq_len
    ch = SCAN_CHUNK
    n_ch = seq_len // ch
    c0, n0, m0, s0 = state
    mat = pl.BlockSpec((None, None, 2, N_HEADS, HEAD_DIM, HEAD_DIM), lambda b: (b, layer, 0, 0, 0, 0))
    in_specs = (
        [pl.BlockSpec((seq_len, N_MIX), lambda b: (b, 0)),
         pl.BlockSpec((n_ch, N_GATE_COLS, ch), lambda b: (b, 0, 0))]
        + _mix_param_specs(ch, None)
        + [mat, pl.BlockSpec((None, None, 2 * N_HEADS, HEAD_DIM), lambda b: (b, layer, 0, 0)),
           pl.BlockSpec(memory_space=pltpu.SMEM), mat])
    args = [p, gr] + _mix_param_args(params) + [
        c0, n0.reshape(n0.shape[0], n0.shape[1], 2 * N_HEADS, HEAD_DIM), m0, s0]
    scratch = _mix_scratch(seq_len, ch) + [pltpu.VMEM((seq_len, HEAD_DIM), F32),
                                           pltpu.VMEM((seq_len, HEAD_DIM), F32)]
    return pl.pallas_call(
        functools.partial(_seqmix_latent_kernel, seq_len=seq_len, ch=ch, layer=layer),
        grid=(bsz,),
        in_specs=in_specs,
        out_specs=pl.BlockSpec((seq_len, 3 * BRANCH_W), lambda b: (b, 0)),
        out_shape=jax.ShapeDtypeStruct((rows, 3 * BRANCH_W), BF16),
        scratch_shapes=scratch,
        compiler_params=pltpu.CompilerParams(dimension_semantics=("arbitrary",)),
        name="seq_mix_latent",
    )(*args)


def _mixer_context_kernel(*refs, seq_len, ch, layer, n_carried):
    x_ref, mod_ref, w_ref, wgt_ref = refs[:4]
    prm = refs[4:4 + N_MIX_PARAMS]
    pos = 4 + N_MIX_PARAMS + n_carried
    hm_ref = refs[pos]
    state_out = refs[pos + 1:pos + 5]
    p_cur, p_next, gr_cur, gr_next, h_scr, upad, shf, dsc = refs[pos + 5:pos + 13]
    s_idx = pl.program_id(0)

    def hand_over():
        for jc in range(N_MIX // PROJ_TILE):
            cs = slice(jc * PROJ_TILE, (jc + 1) * PROJ_TILE)
            p_cur[:, cs] = p_next[:, cs]
        gr_cur[...] = gr_next[...]

    @pl.when(s_idx == 0)
    def _():
        _init_tables(prm[6], dsc, None, seq_len=seq_len, ch=ch, scale=HEAD_DIM ** -0.5)
        _project(x_ref, mod_ref, w_ref, wgt_ref, p_next, gr_next, h_scr, ch=ch)
        hand_over()

    @pl.when(s_idx > 0)
    def _():
        pieces = _project_pieces(x_ref, mod_ref, w_ref, wgt_ref, p_next, gr_next, h_scr, ch=ch)
        pieces[0]()
        _mix_context_sequence(p_cur, gr_cur, prm, hm_ref, state_out, upad, shf, dsc,
                              seq_len=seq_len, fillers=pieces[1:])
        hand_over()


def _mixer_context(x2d, mod, seq_len, layer, depth, wp, wgt, params, carried):
    rows = x2d.shape[0]
    bsz = rows // seq_len
    assert seq_len == SCAN_CHUNK, "context sequences are mixed as a single scan chunk"
    ch = seq_len
    n_ch = 1

    def nxt(s):
        return jnp.minimum(s, bsz - 1)

    def cur(s):
        return jnp.maximum(s - 1, 0)

    mat = pl.BlockSpec((None, None, 2, N_HEADS, HEAD_DIM, HEAD_DIM), lambda s: (cur(s), layer, 0, 0, 0, 0))
    vec = pl.BlockSpec((None, None, 2 * N_HEADS, HEAD_DIM), lambda s: (cur(s), layer, 0, 0))
    carried = [] if carried is None else list(carried)
    n_fixed = 4 + N_MIX_PARAMS
    in_specs = (
        [pl.BlockSpec((seq_len, D_MODEL), lambda s: (nxt(s), 0)),
         pl.BlockSpec((None, 6, D_MODEL), lambda s: (0, 0, 0)),
         _weight_spec((D_MODEL, N_MIX), layer),
         _weight_spec((N_GATE_COLS, D_MODEL), layer)]
        + _mix_param_specs(ch, None)
        + [pl.BlockSpec(memory_space=pl.ANY)] * len(carried))
    out_specs = [pl.BlockSpec((seq_len, 3 * BRANCH_W), lambda s: (cur(s), 0)), mat, vec, vec, mat]
    out_shape = [
        jax.ShapeDtypeStruct((rows, 3 * BRANCH_W), BF16),
        jax.ShapeDtypeStruct((bsz, depth, 2, N_HEADS, HEAD_DIM, HEAD_DIM), F32),
        jax.ShapeDtypeStruct((bsz, depth, 2 * N_HEADS, HEAD_DIM), F32),
        jax.ShapeDtypeStruct((bsz, depth, 2 * N_HEADS, LANES), F32),
        jax.ShapeDtypeStruct((bsz, depth, 2, N_HEADS, HEAD_DIM, HEAD_DIM), F32),
    ]
    scratch = [
        pltpu.VMEM((seq_len, N_MIX), BF16),
        pltpu.VMEM((seq_len, N_MIX), BF16),
        pltpu.VMEM((n_ch, N_GATE_COLS, ch), F32),
        pltpu.VMEM((n_ch, N_GATE_COLS, ch), F32),
        pltpu.VMEM((seq_len, D_MODEL), BF16),
        pltpu.VMEM((seq_len + 32, BRANCH_W), F32),
        pltpu.VMEM((SUBLANES, CONV_WIN, BRANCH_W), F32),
        pltpu.VMEM((N_HEADS, ch, ch), F32),
    ]
    return pl.pallas_call(
        functools.partial(_mixer_context_kernel, seq_len=seq_len, ch=ch, layer=layer, n_carried=len(carried)),
        grid=(bsz + 1,),
        in_specs=in_specs,
        out_specs=out_specs,
        out_shape=out_shape,
        scratch_shapes=scratch,
        input_output_aliases={n_fixed + i: 1 + i for i in range(len(carried))},
        compiler_params=pltpu.CompilerParams(dimension_semantics=("arbitrary",)),
        name="mixer_context",
    )(x2d, mod, wp, wgt, *_mix_param_args(params), *carried)


def _post_kernel(hm_ref, x_ref, mod_ref, wgm_ref, wa_ref, wb_ref, wc_ref, wo_ref,
                 l1w_ref, l1b_ref, w13_ref, w2_ref, l2w_ref, l2b_ref, o_ref, *, alpha):
    shift1 = mod_ref[0:1, :]
    scale1 = mod_ref[1:2, :]
    gate1 = mod_ref[2:3, :]
    shift2 = mod_ref[3:4, :]
    scale2 = mod_ref[4:5, :]
    gate2 = mod_ref[5:6, :]
    x = x_ref[...]
    h1 = (x * (1.0 + scale1) + shift1).astype(BF16)
    merged = None
    for g, w_ref in enumerate((wa_ref, wb_ref, wc_ref)):
        gate = jax.nn.sigmoid(_dot(h1, wgm_ref[:, g * D_MODEL:(g + 1) * D_MODEL]))
        term = gate * _dot(hm_ref[:, g * BRANCH_W:(g + 1) * BRANCH_W], w_ref[...])
        merged = term if merged is None else merged + term
    mix = _dot(merged.astype(BF16), wo_ref[...])
    x1 = _layer_norm(alpha * x + gate1 * mix, l1w_ref[...], l1b_ref[...])
    h2 = (x1 * (1.0 + scale2) + shift2).astype(BF16)
    ff = jnp.zeros(x1.shape, F32)
    for c in range(FFN_HIDDEN // FFN_TILE):
        a = _dot(h2, w13_ref[:, c * FFN_TILE:(c + 1) * FFN_TILE])
        gt = _dot(h2, w13_ref[:, FFN_HIDDEN + c * FFN_TILE:FFN_HIDDEN + (c + 1) * FFN_TILE])
        act = (gt * jax.nn.sigmoid(gt)) * a
        ff = ff + _dot(act.astype(BF16), w2_ref[c * FFN_TILE:(c + 1) * FFN_TILE, :])
    o_ref[...] = _layer_norm(alpha * x1 + gate2 * ff, l2w_ref[...], l2b_ref[...])


def _post(hm, x2d, mod, seq_len, layer, w, alpha):
    rows = x2d.shape[0]
    tm = ROW_TILE
    n_seq = mod.shape[0]

    def seq_of(i):
        return (i * tm) // seq_len if n_seq > 1 else 0

    vec = _weight_spec((1, D_MODEL), layer)
    return pl.pallas_call(
        functools.partial(_post_kernel, alpha=alpha),
        grid=(rows // tm,),
        in_specs=[
            pl.BlockSpec((tm, 3 * BRANCH_W), lambda i: (i, 0)),
            pl.BlockSpec((tm, D_MODEL), lambda i: (i, 0)),
            pl.BlockSpec((None, 6, D_MODEL), lambda i: (seq_of(i), 0, 0)),
            _weight_spec((D_MODEL, N_GM), layer),
            _weight_spec((BRANCH_W, D_MODEL), layer), _weight_spec((BRANCH_W, D_MODEL), layer),
            _weight_spec((BRANCH_W, D_MODEL), layer),
            _weight_spec((D_MODEL, D_MODEL), layer), vec, vec,
            _weight_spec((D_MODEL, 2 * FFN_HIDDEN), layer), _weight_spec((FFN_HIDDEN, D_MODEL), layer), vec, vec,
        ],
        out_specs=pl.BlockSpec((tm, D_MODEL), lambda i: (i, 0)),
        out_shape=jax.ShapeDtypeStruct((rows, D_MODEL), F32),
        compiler_params=pltpu.CompilerParams(dimension_semantics=("parallel",)),
        name="post",
    )(hm, x2d, mod, w["gm_w"], w["mlstm_out_w"], w["conv_out_w"], w["ret_out_w"], w["out_w"],
      w["ln1_w"], w["ln1_b"], w["ffn_w13"], w["ffn_w2"], w["ln2_w"], w["ln2_b"])


def kernel(x_prompt, x_sample, state_mlstm_C, state_mlstm_n, state_mlstm_m, state_ret_S, c, c_ctx, ada_w, ada_b, in_w, mlstm_gate_b, mlstm_norm_w, mlstm_out_w, conv_w, conv_b, conv_ln_w, conv_ln_b, conv_out_w, ret_decay, ret_norm_w, ret_out_w, out_w, ln1_w, ln1_b, ln2_w, ln2_b, ffn_w13, ffn_w2):
    depth = in_w.shape[0]
    alpha = (2.0 * depth) ** 0.25
    bsz, seq, _ = x_prompt.shape
    dbsz, dseq, _ = x_sample.shape
    ch = SCAN_CHUNK
    wp, gm_w, wgt = _split_in_w(in_w)
    dense = {
        "gm_w": gm_w,
        "mlstm_out_w": mlstm_out_w.astype(BF16), "conv_out_w": conv_out_w.astype(BF16),
        "ret_out_w": ret_out_w.astype(BF16), "out_w": out_w.astype(BF16),
        "ffn_w13": ffn_w13.astype(BF16), "ffn_w2": ffn_w2.astype(BF16),
        "ln1_w": ln1_w.reshape(depth, 1, D_MODEL), "ln1_b": ln1_b.reshape(depth, 1, D_MODEL),
        "ln2_w": ln2_w.reshape(depth, 1, D_MODEL), "ln2_b": ln2_b.reshape(depth, 1, D_MODEL),
    }
    gate_b_flat = mlstm_gate_b.reshape(depth, N_GATE_COLS)

    c_all = jnp.concatenate([c_ctx[None, :], c, jnp.zeros((8 - 1 - dbsz, D_MODEL), F32)], axis=0)
    mod = _ada_mod(c_all, ada_w, ada_b)

    y_prompt = x_prompt.reshape(bsz * seq, D_MODEL)
    y_sample = x_sample.reshape(dbsz * dseq, D_MODEL)
    states = [jnp.zeros((bsz, depth, 2, N_HEADS, HEAD_DIM, HEAD_DIM), F32),
              jnp.zeros((bsz, depth, 2 * N_HEADS, HEAD_DIM), F32),
              jnp.zeros((bsz, depth, 2 * N_HEADS, LANES), F32),
              jnp.zeros((bsz, depth, 2, N_HEADS, HEAD_DIM, HEAD_DIM), F32)]
    for l in range(depth):
        mix_params = {
            "gate_b_row": gate_b_flat[l][:, None],
            "mlstm_norm_w": mlstm_norm_w[l][None, :],
            "conv_w": jnp.pad(conv_w[l], ((0, 32 - CONV_K), (0, 0))),
            "conv_b": conv_b[l][None, :],
            "conv_ln_w": conv_ln_w[l][None, :],
            "conv_ln_b": conv_ln_b[l][None, :],
            "ret_decay": jnp.broadcast_to(ret_decay[l].reshape(2 * N_HEADS, 1), (2 * N_HEADS, ch)),
            "ret_norm_w": ret_norm_w[l][None, :],
        }
        mod_ctx = mod[l, 0:1].reshape(1, 6, D_MODEL)
        mod_lat = mod[l, 1:1 + dbsz].reshape(dbsz, 6, D_MODEL)

        hm, *states = _mixer_context(y_prompt, mod_ctx, seq, l, depth, wp, wgt, mix_params, states)
        y_prompt = _post(hm, y_prompt, mod_ctx, seq, l, dense, alpha)

        p, gr = _in_proj(y_sample, mod_lat, dseq, l, wp, wgt)
        hm = _seq_mix_latent(p, gr, dseq, l, mix_params,
                             (state_mlstm_C, state_mlstm_n, state_mlstm_m, state_ret_S))
        y_sample = _post(hm, y_sample, mod_lat, dseq, l, dense, alpha)

    new_c, new_n, new_m, new_s = states
    return (y_prompt.reshape(bsz, seq, D_MODEL), y_sample.reshape(dbsz, dseq, D_MODEL),
            new_c, new_n.reshape(bsz, depth, 2, N_HEADS, HEAD_DIM),
            new_m[:, :, :, 0].reshape(bsz, depth, 2, N_HEADS), new_s)
```
